```python
import jax, jax.numpy as jnp
from jax import lax
import numpy as np

D_MODEL = 4096
BATCH = 16
SEQ = 256
DEPTH = 4
DEC_BATCH = 8
DEC_SEQ = 1024
PAST_LEN = 512

GRID_W = 64
N_MIXERS = 2
HEAD_DIM = 128
N_HEADS = D_MODEL // HEAD_DIM
N_KV_HEADS = N_HEADS // 4
KV_GROUP = N_HEADS // N_KV_HEADS
WINDOW = 128
BLOCK = 128
ROPE_AXIS_DIM = HEAD_DIM // 2
ROPE_BASE = 10000.0
N_FOURIER_GROUPS = 8
FOURIER_GROUP_DIM = D_MODEL // N_FOURIER_GROUPS
D_FF = -(-8 * D_MODEL // (3 * 256)) * 256
N_ATTN_LAYERS = (DEPTH + N_MIXERS - 1) // N_MIXERS
N_FOURIER_LAYERS = DEPTH // N_MIXERS
EPS = 1e-6
NEG = -1e30

kernel_name = "hybrid_swa_fnet_diffusion_step"


def rms_norm(x, g):
    xf = x.astype(jnp.float32)
    y = xf * lax.rsqrt(jnp.mean(xf * xf, axis=-1, keepdims=True) + EPS)
    return (y * g.astype(jnp.float32)).astype(x.dtype)


def modulate(h, shift, scale):
    return h * (1 + scale) + shift


def axial_rope(x):
    L = x.shape[1]
    rows = L // GRID_W
    row = jnp.repeat(jnp.arange(rows, dtype=jnp.float32), GRID_W)
    col = jnp.tile(jnp.arange(GRID_W, dtype=jnp.float32), rows)
    inv_freq = ROPE_BASE ** (-jnp.arange(0, ROPE_AXIS_DIM, 2, dtype=jnp.float32) / ROPE_AXIS_DIM)

    def rot(xh, pos):
        ang = pos[:, None] * inv_freq[None, :]
        cos = jnp.cos(ang)[None, :, None, :]
        sin = jnp.sin(ang)[None, :, None, :]
        x1, x2 = jnp.split(xh, 2, axis=-1)
        return jnp.concatenate([x1 * cos - x2 * sin, x1 * sin + x2 * cos], axis=-1)

    xf = x.astype(jnp.float32)
    out = jnp.concatenate([rot(xf[..., :ROPE_AXIS_DIM], row), rot(xf[..., ROPE_AXIS_DIM:], col)], axis=-1)
    return out.astype(x.dtype)


def project_qkv(h, w):
    B, L, _ = h.shape
    qkv = h @ w
    nq = N_HEADS * HEAD_DIM
    nk = N_KV_HEADS * HEAD_DIM
    q = qkv[..., :nq].reshape(B, L, N_HEADS, HEAD_DIM)
    k = qkv[..., nq:nq + nk].reshape(B, L, N_KV_HEADS, HEAD_DIM)
    v = qkv[..., nq + nk:].reshape(B, L, N_KV_HEADS, HEAD_DIM)
    return q, k, v


def attend_context(q, k, v, sink):
    B, L = q.shape[0], q.shape[1]
    scale = HEAD_DIM ** -0.5
    qg = q.reshape(B, L, N_KV_HEADS, KV_GROUP, HEAD_DIM)
    s = jnp.einsum("bqkgd,bskd->bkgqs", qg, k, preferred_element_type=jnp.float32) * scale
    sk = sink.astype(jnp.float32).reshape(N_KV_HEADS, KV_GROUP)[None, :, :, None]
    m = jnp.maximum(jnp.max(s, axis=-1), sk)
    p = jnp.exp(s - m[..., None])
    denom = jnp.sum(p, axis=-1) + jnp.exp(sk - m)
    o = jnp.einsum("bkgqs,bskd->bqkgd", p, v.astype(jnp.float32))
    o = o / jnp.transpose(denom, (0, 3, 1, 2))[..., None]
    return o.reshape(B, L, N_HEADS * HEAD_DIM).astype(q.dtype)


def attend_latent(q, k, v, ck, cv, sink):
    B, L = q.shape[0], q.shape[1]
    nb = L // BLOCK
    scale = HEAD_DIM ** -0.5
    qb = q.reshape(B, nb, BLOCK, N_KV_HEADS, KV_GROUP, HEAD_DIM)

    def band(t):
        tp = jnp.pad(t, ((0, 0), (BLOCK, BLOCK), (0, 0), (0, 0))).reshape(B, nb + 2, BLOCK, N_KV_HEADS, HEAD_DIM)
        return jnp.concatenate([tp[:, :-2], tp[:, 1:-1], tp[:, 2:]], axis=2)

    kw = band(k)
    vw = band(v)
    qi = jnp.arange(BLOCK)[:, None]
    kj = jnp.arange(3 * BLOCK)[None, :]
    in_band = jnp.abs(kj - BLOCK - qi) <= WINDOW
    kpos = jnp.arange(nb)[:, None, None] * BLOCK - BLOCK + kj[None]
    valid = in_band[None] & (kpos >= 0) & (kpos < L)

    s_w = jnp.einsum("bnqkgd,bnskd->bnkgqs", qb, kw, preferred_element_type=jnp.float32) * scale
    s_w = jnp.where(valid[None, :, None, None], s_w, NEG)
    s_c = jnp.einsum("bnqkgd,bskd->bnkgqs", qb, ck, preferred_element_type=jnp.float32) * scale
    sk = sink.astype(jnp.float32).reshape(N_KV_HEADS, KV_GROUP)[None, None, :, :, None]
    m = jnp.maximum(jnp.maximum(jnp.max(s_w, axis=-1), jnp.max(s_c, axis=-1)), sk)
    p_w = jnp.exp(s_w - m[..., None])
    p_c = jnp.exp(s_c - m[..., None])
    denom = jnp.sum(p_w, axis=-1) + jnp.sum(p_c, axis=-1) + jnp.exp(sk - m)
    o = (jnp.einsum("bnkgqs,bnskd->bnqkgd", p_w, vw.astype(jnp.float32))
         + jnp.einsum("bnkgqs,bskd->bnqkgd", p_c, cv.astype(jnp.float32)))
    o = o / jnp.transpose(denom, (0, 1, 4, 2, 3))[..., None]
    return o.reshape(B, L, N_HEADS * HEAD_DIM).astype(q.dtype)


def fourier_mix(h, w):
    B, L, _ = h.shape
    hg = h.astype(jnp.float32).reshape(B, L, N_FOURIER_GROUPS, FOURIER_GROUP_DIM)
    f = jnp.fft.fft2(hg, axes=(1, 3), norm="ortho").real
    return f.reshape(B, L, D_MODEL).astype(h.dtype) @ w


def swiglu(h, w_in, w_out):
    gu = h @ w_in
    g, u = jnp.split(gu, 2, axis=-1)
    return (jax.nn.silu(g) * u) @ w_out


def setup_inputs(seed: int = 0) -> dict:
    key = jax.random.key(seed)
    ks = jax.random.split(key, 20)
    D = D_MODEL
    qkv_w = (N_HEADS + 2 * N_KV_HEADS) * HEAD_DIM
    nrm = jax.random.normal
    return {
        "x_prompt": nrm(ks[0], (BATCH, SEQ, D), jnp.float32),
        "x_sample": nrm(ks[1], (DEC_BATCH, DEC_SEQ, D), jnp.float32),
        "cache_k": nrm(ks[2], (DEC_BATCH, N_ATTN_LAYERS, PAST_LEN, N_KV_HEADS, HEAD_DIM), jnp.float32),
        "cache_v": nrm(ks[3], (DEC_BATCH, N_ATTN_LAYERS, PAST_LEN, N_KV_HEADS, HEAD_DIM), jnp.float32),
        "c": nrm(ks[4], (DEC_BATCH, D), jnp.float32),
        "c_ctx": nrm(ks[5], (D,), jnp.float32),
        "w_ada": nrm(ks[6], (DEPTH, D, 6 * D), jnp.float32) * (0.5 * D ** -0.5),
        "b_ada": nrm(ks[7], (DEPTH, 6 * D), jnp.float32) * 0.02,
        "g_mix_pre": 1.0 + 0.05 * nrm(ks[8], (DEPTH, D), jnp.float32),
        "g_mix_post": 1.0 + 0.05 * nrm(ks[9], (DEPTH, D), jnp.float32),
        "g_ffn_pre": 1.0 + 0.05 * nrm(ks[10], (DEPTH, D), jnp.float32),
        "g_ffn_post": 1.0 + 0.05 * nrm(ks[11], (DEPTH, D), jnp.float32),
        "w_qkv": nrm(ks[12], (N_ATTN_LAYERS, D, qkv_w), jnp.float32) * D ** -0.5,
        "w_attn_out": nrm(ks[13], (N_ATTN_LAYERS, N_HEADS * HEAD_DIM, D), jnp.float32) * (N_HEADS * HEAD_DIM) ** -0.5,
        "attn_sink": nrm(ks[14], (N_ATTN_LAYERS, N_HEADS), jnp.float32) * 0.5,
        "w_fourier": nrm(ks[15], (N_FOURIER_LAYERS, D, D), jnp.float32) * D ** -0.5,
        "w_ffn_in": nrm(ks[16], (DEPTH, D, 2 * D_FF), jnp.float32) * D ** -0.5,
        "w_ffn_out": nrm(ks[17], (DEPTH, D_FF, D), jnp.float32) * D_FF ** -0.5,
    }


def reference(x_prompt, x_sample, cache_k, cache_v, c, c_ctx, w_ada, b_ada, g_mix_pre, g_mix_post,
              g_ffn_pre, g_ffn_post, w_qkv, w_attn_out, attn_sink, w_fourier, w_ffn_in, w_ffn_out):
    y_p = x_prompt
    y_s = x_sample
    silu_ctx = jax.nn.silu(c_ctx)
    silu_c = jax.nn.silu(c)
    new_k = []
    new_v = []
    for i in range(DEPTH):
        mod_p = (silu_ctx @ w_ada[i] + b_ada[i])[None, None, :]
        mod_s = (silu_c @ w_ada[i] + b_ada[i])[:, None, :]
        sh1_p, sc1_p, ga1_p, sh2_p, sc2_p, ga2_p = jnp.split(mod_p, 6, axis=-1)
        sh1_s, sc1_s, ga1_s, sh2_s, sc2_s, ga2_s = jnp.split(mod_s, 6, axis=-1)

        h_p = modulate(rms_norm(y_p, g_mix_pre[i]), sh1_p, sc1_p)
        h_s = modulate(rms_norm(y_s, g_mix_pre[i]), sh1_s, sc1_s)
        if i % N_MIXERS == 0:
            a = i // N_MIXERS
            q_p, k_p, v_p = project_qkv(h_p, w_qkv[a])
            o_p = attend_context(q_p, k_p, v_p, attn_sink[a]) @ w_attn_out[a]
            new_k.append(k_p)
            new_v.append(v_p)
            q_s, k_s, v_s = project_qkv(h_s, w_qkv[a])
            q_s = axial_rope(q_s)
            k_s = axial_rope(k_s)
            o_s = attend_latent(q_s, k_s, v_s, cache_k[:, a], cache_v[:, a], attn_sink[a]) @ w_attn_out[a]
        else:
            f = i // N_MIXERS
            o_p = fourier_mix(h_p, w_fourier[f])
            o_s = fourier_mix(h_s, w_fourier[f])
        y_p = y_p + ga1_p * rms_norm(o_p, g_mix_post[i])
        y_s = y_s + ga1_s * rms_norm(o_s, g_mix_post[i])

        h_p = modulate(rms_norm(y_p, g_ffn_pre[i]), sh2_p, sc2_p)
        h_s = modulate(rms_norm(y_s, g_ffn_pre[i]), sh2_s, sc2_s)
        y_p = y_p + ga2_p * rms_norm(swiglu(h_p, w_ffn_in[i], w_ffn_out[i]), g_ffn_post[i])
        y_s = y_s + ga2_s * rms_norm(swiglu(h_s, w_ffn_in[i], w_ffn_out[i]), g_ffn_post[i])

    new_cache_k = jnp.stack(new_k, axis=1)
    new_cache_v = jnp.stack(new_v, axis=1)
    return (y_p, y_s, new_cache_k, new_cache_v)
```

```python
import functools
import math

import jax
import jax.numpy as jnp
from jax import lax
from jax.experimental import pallas as pl
from jax.experimental.pallas import tpu as pltpu

F32 = jnp.float32
BF16 = jnp.bfloat16

D_MODEL = 4096
N_CTX_REQ = 16
CTX_LEN = 256
N_LAT_REQ = 8
LAT_LEN = 1024
PAST_LEN = 512
DEPTH = 4
GRID_W = 64
HEAD_DIM = 128
N_HEADS = 32
N_KV_HEADS = 8
KV_GROUP = 4
ATTN_BLOCK = 128
ROPE_BASE = 10000.0
N_GROUPS = 8
GROUP_DIM = D_MODEL // N_GROUPS
D_FF = 11008
EPS = 1e-6
NEG = -1e30

T_CTX = N_CTX_REQ * CTX_LEN
T_LAT = N_LAT_REQ * LAT_LEN
T_ALL = T_CTX + T_LAT
MOD_ROWS = 16
CTX_MOD_ROW = N_LAT_REQ
QKV_W = (N_HEADS + 2 * N_KV_HEADS) * HEAD_DIM
ATTN_SCALE = HEAD_DIM ** -0.5

V7X_VMEM_BYTES = 64 * 1024 * 1024
MIB = 1024 * 1024

ELEM_ROWS = 256
MM_ROWS = 1024
FFN_OUT_ROWS = 512
FFN_TILE = 256


def _params(vmem_bytes, ngrid):
    assert vmem_bytes <= V7X_VMEM_BYTES
    return pltpu.CompilerParams(
        dimension_semantics=("arbitrary",) * ngrid,
        vmem_limit_bytes=int(vmem_bytes),
    )


def _mod_row_of_block(i, rows_per_block):
    ctx_blocks = T_CTX // rows_per_block
    blocks_per_lat = LAT_LEN // rows_per_block
    return jnp.where(i < ctx_blocks, CTX_MOD_ROW, (i - ctx_blocks) // blocks_per_lat)


ADA_TILE = 512


def _ada_kernel(c_ref, w_ref, b_ref, o_ref):
    c = c_ref[...]
    s = (c * jax.nn.sigmoid(c)).astype(BF16)
    w = w_ref[0].astype(BF16)
    o_ref[0] = jnp.dot(s, w, preferred_element_type=F32) + b_ref[0]


def _ada_table(cond, w_ada, b_ada):
    n_out = 6 * D_MODEL
    vmem = 2 * (D_MODEL * ADA_TILE * 4) + D_MODEL * ADA_TILE * 2 + 4 * MIB
    return pl.pallas_call(
        _ada_kernel,
        grid=(DEPTH, n_out // ADA_TILE),
        in_specs=[
            pl.BlockSpec((MOD_ROWS, D_MODEL), lambda l, j: (0, 0)),
            pl.BlockSpec((1, D_MODEL, ADA_TILE), lambda l, j: (l, 0, j)),
            pl.BlockSpec((1, 1, ADA_TILE), lambda l, j: (l, 0, j)),
        ],
        out_specs=pl.BlockSpec((1, MOD_ROWS, ADA_TILE), lambda l, j: (l, 0, j)),
        out_shape=jax.ShapeDtypeStruct((DEPTH, MOD_ROWS, n_out), F32),
        compiler_params=_params(vmem, 2),
        name="ada_table",
    )(cond, w_ada, b_ada.reshape(DEPTH, 1, n_out))


def _rms(x, g):
    return x * lax.rsqrt(jnp.mean(x * x, axis=-1, keepdims=True) + EPS) * g


def _pre(y, g_pre, mod_ref, sub):
    shift = mod_ref[pl.ds(3 * sub, 1), :]
    scale = mod_ref[pl.ds(3 * sub + 1, 1), :]
    return _rms(y, g_pre) * (1.0 + scale) + shift


def _pre_kernel(sub, y_ref, gpre_ref, mod_ref, h_ref):
    h_ref[...] = _pre(y_ref[...], gpre_ref[...], mod_ref, sub).astype(h_ref.dtype)


def _post_kernel(sub, o_ref, y_ref, gpost_ref, mod_ref, yo_ref):
    gate = mod_ref[pl.ds(3 * sub + 2, 1), :]
    yo_ref[...] = y_ref[...] + gate * _rms(o_ref[...], gpost_ref[...])


def _postpre_kernel(sub, nsub, o_ref, y_ref, gpost_ref, mod_ref, gpre_ref, nmod_ref, yo_ref, h_ref):
    gate = mod_ref[pl.ds(3 * sub + 2, 1), :]
    yn = y_ref[...] + gate * _rms(o_ref[...], gpost_ref[...])
    yo_ref[...] = yn
    h_ref[...] = _pre(yn, gpre_ref[...], nmod_ref, nsub).astype(h_ref.dtype)


def _row_spec():
    return pl.BlockSpec((ELEM_ROWS, D_MODEL), lambda i: (i, 0))


def _gain_spec(layer):
    return pl.BlockSpec((None, 1, D_MODEL), lambda i: (layer, 0, 0))


def _mod_spec(layer):
    return pl.BlockSpec((None, None, 6, D_MODEL),
                        lambda i: (layer, _mod_row_of_block(i, ELEM_ROWS), 0, 0))


_ELEM_BLOCK_F32 = ELEM_ROWS * D_MODEL * 4


def _pre_call(y, g_pre, mod, layer, sub):
    vmem = 2 * (_ELEM_BLOCK_F32 + _ELEM_BLOCK_F32 // 2) + 3 * _ELEM_BLOCK_F32 + 2 * MIB
    return pl.pallas_call(
        functools.partial(_pre_kernel, sub),
        grid=(T_ALL // ELEM_ROWS,),
        in_specs=[_row_spec(), _gain_spec(layer), _mod_spec(layer)],
        out_specs=_row_spec(),
        out_shape=jax.ShapeDtypeStruct((T_ALL, D_MODEL), BF16),
        compiler_params=_params(vmem, 1),
        name="pre_norm",
    )(y, g_pre, mod)


def _post_call(o, y, g_post, mod, layer, sub):
    vmem = 2 * 3 * _ELEM_BLOCK_F32 + 3 * _ELEM_BLOCK_F32 + 2 * MIB
    return pl.pallas_call(
        functools.partial(_post_kernel, sub),
        grid=(T_ALL // ELEM_ROWS,),
        in_specs=[_row_spec(), _row_spec(), _gain_spec(layer), _mod_spec(layer)],
        out_specs=_row_spec(),
        out_shape=jax.ShapeDtypeStruct((T_ALL, D_MODEL), F32),
        compiler_params=_params(vmem, 1),
        name="post_norm",
    )(o, y, g_post, mod)


def _postpre_call(o, y, g_post, mod, layer, sub, g_pre, nlayer, nsub):
    vmem = 2 * (3 * _ELEM_BLOCK_F32 + _ELEM_BLOCK_F32 // 2) + 4 * _ELEM_BLOCK_F32 + 2 * MIB
    return pl.pallas_call(
        functools.partial(_postpre_kernel, sub, nsub),
        grid=(T_ALL // ELEM_ROWS,),
        in_specs=[_row_spec(), _row_spec(), _gain_spec(layer), _mod_spec(layer),
                  _gain_spec(nlayer), _mod_spec(nlayer)],
        out_specs=[_row_spec(), _row_spec()],
        out_shape=[jax.ShapeDtypeStruct((T_ALL, D_MODEL), F32),
                   jax.ShapeDtypeStruct((T_ALL, D_MODEL), BF16)],
        compiler_params=_params(vmem, 1),
        name="post_pre_norm",
    )(o, y, g_post, mod, g_pre, mod)


def _mm_kernel(x_ref, w_ref, o_ref):
    o_ref[...] = jnp.dot(x_ref[...], w_ref[...], preferred_element_type=F32).astype(o_ref.dtype)


def _matmul(x, w, *, rows, cols, out_dtype, name):
    m, k = x.shape
    n = w.shape[1]
    out_bytes = jnp.dtype(out_dtype).itemsize
    vmem = 2 * (rows * k * 2 + k * cols * 2 + rows * cols * out_bytes) + 2 * rows * cols * 4 + 2 * MIB
    return pl.pallas_call(
        _mm_kernel,
        grid=(m // rows, n // cols),
        in_specs=[pl.BlockSpec((rows, k), lambda i, j: (i, 0)),
                  pl.BlockSpec((k, cols), lambda i, j: (0, j))],
        out_specs=pl.BlockSpec((rows, cols), lambda i, j: (i, j)),
        out_shape=jax.ShapeDtypeStruct((m, n), out_dtype),
        compiler_params=_params(vmem, 2),
        name=name,
    )(x, w)


def _swiglu_in_kernel(x_ref, wg_ref, wu_ref, o_ref):
    x = x_ref[...]
    g = jnp.dot(x, wg_ref[...], preferred_element_type=F32)
    u = jnp.dot(x, wu_ref[...], preferred_element_type=F32)
    o_ref[...] = (g * jax.nn.sigmoid(g) * u).astype(o_ref.dtype)


def _swiglu_in(h, w_in):
    n_tiles = D_FF // FFN_TILE
    vmem = (2 * (MM_ROWS * D_MODEL * 2 + 2 * D_MODEL * FFN_TILE * 2 + MM_ROWS * FFN_TILE * 2)
            + 3 * MM_ROWS * FFN_TILE * 4 + 2 * MIB)
    return pl.pallas_call(
        _swiglu_in_kernel,
        grid=(T_ALL // MM_ROWS, n_tiles),
        in_specs=[pl.BlockSpec((MM_ROWS, D_MODEL), lambda i, j: (i, 0)),
                  pl.BlockSpec((D_MODEL, FFN_TILE), lambda i, j: (0, j)),
                  pl.BlockSpec((D_MODEL, FFN_TILE), lambda i, j: (0, n_tiles + j))],
        out_specs=pl.BlockSpec((MM_ROWS, FFN_TILE), lambda i, j: (i, j)),
        out_shape=jax.ShapeDtypeStruct((T_ALL, D_FF), BF16),
        compiler_params=_params(vmem, 2),
        name="ffn_in_swiglu",
    )(h, w_in, w_in)


def _dot_nt(a, b):
    return lax.dot_general(a, b, (((1,), (1,)), ((), ())), preferred_element_type=F32)


def _attn_ctx_kernel(sink_ref, q_ref, k_ref, v_ref, prev_ref, o_ref):
    del prev_ref
    kh = pl.program_id(1)
    k = k_ref[...].astype(BF16)
    v = v_ref[...].astype(BF16)
    for g in range(KV_GROUP):
        cols = slice(g * HEAD_DIM, (g + 1) * HEAD_DIM)
        q = (q_ref[:, cols] * ATTN_SCALE).astype(BF16)
        s = _dot_nt(q, k)
        sk = sink_ref[kh * KV_GROUP + g]
        m = jnp.maximum(jnp.max(s, axis=-1, keepdims=True), sk)
        p = jnp.exp(s - m)
        denom = jnp.sum(p, axis=-1, keepdims=True) + jnp.exp(sk - m)
        o = jnp.dot(p.astype(BF16), v, preferred_element_type=F32)
        o_ref[:, cols] = (o / denom).astype(o_ref.dtype)


def _rope(x, cos, sin_lo, sin_hi):
    return x * cos + pltpu.roll(x, 96, 1) * sin_lo + pltpu.roll(x, 32, 1) * sin_hi


def _attn_lat_kernel(sink_ref, q_ref, k_ref, v_ref, ck_ref, cv_ref, cos_ref, slo_ref, shi_ref,
                     o_ref, kr_ref, vb_ref, ckb_ref, cvb_ref):
    kh = pl.program_id(1)
    n_blocks = LAT_LEN // ATTN_BLOCK
    kr_ref[...] = _rope(k_ref[...], cos_ref[...], slo_ref[...], shi_ref[...]).astype(BF16)
    vb_ref[...] = v_ref[...].astype(BF16)
    ckb_ref[...] = ck_ref[...].astype(BF16)
    cvb_ref[...] = cv_ref[...].astype(BF16)

    rows = KV_GROUP * ATTN_BLOCK
    qi = lax.broadcasted_iota(jnp.int32, (rows, ATTN_BLOCK), 0) % ATTN_BLOCK
    kj = lax.broadcasted_iota(jnp.int32, (rows, ATTN_BLOCK), 1)
    band_prev = kj >= qi
    band_next = kj <= qi
    sk = jnp.concatenate(
        [jnp.full((ATTN_BLOCK, 1), sink_ref[kh * KV_GROUP + g], F32) for g in range(KV_GROUP)], axis=0)

    def body(n, carry):
        r0 = pl.multiple_of(n * ATTN_BLOCK, ATTN_BLOCK)
        rp = pl.multiple_of(jnp.maximum(n - 1, 0) * ATTN_BLOCK, ATTN_BLOCK)
        rn = pl.multiple_of(jnp.minimum(n + 1, n_blocks - 1) * ATTN_BLOCK, ATTN_BLOCK)
        cos = cos_ref[pl.ds(r0, ATTN_BLOCK), :]
        slo = slo_ref[pl.ds(r0, ATTN_BLOCK), :]
        shi = shi_ref[pl.ds(r0, ATTN_BLOCK), :]
        q = jnp.concatenate(
            [(_rope(q_ref[pl.ds(r0, ATTN_BLOCK), g * HEAD_DIM:(g + 1) * HEAD_DIM], cos, slo, shi)
              * ATTN_SCALE).astype(BF16) for g in range(KV_GROUP)], axis=0)
        s_p = jnp.where(band_prev & (n >= 1), _dot_nt(q, kr_ref[pl.ds(rp, ATTN_BLOCK), :]), NEG)
        s_0 = _dot_nt(q, kr_ref[pl.ds(r0, ATTN_BLOCK), :])
        s_n = jnp.where(band_next & (n <= n_blocks - 2), _dot_nt(q, kr_ref[pl.ds(rn, ATTN_BLOCK), :]), NEG)
        s_c = _dot_nt(q, ckb_ref[...])
        m = jnp.maximum(jnp.max(s_p, axis=-1, keepdims=True), jnp.max(s_0, axis=-1, keepdims=True))
        m = jnp.maximum(m, jnp.max(s_n, axis=-1, keepdims=True))
        m = jnp.maximum(m, jnp.max(s_c, axis=-1, keepdims=True))
        m = jnp.maximum(m, sk)
        p_p = jnp.exp(s_p - m)
        p_0 = jnp.exp(s_0 - m)
        p_n = jnp.exp(s_n - m)
        p_c = jnp.exp(s_c - m)
        denom = (jnp.sum(p_p, axis=-1, keepdims=True) + jnp.sum(p_0, axis=-1, keepdims=True)
                 + jnp.sum(p_n, axis=-1, keepdims=True) + jnp.sum(p_c, axis=-1, keepdims=True)
                 + jnp.exp(sk - m))
        o = (jnp.dot(p_p.astype(BF16), vb_ref[pl.ds(rp, ATTN_BLOCK), :], preferred_element_type=F32)
             + jnp.dot(p_0.astype(BF16), vb_ref[pl.ds(r0, ATTN_BLOCK), :], preferred_element_type=F32)
             + jnp.dot(p_n.astype(BF16), vb_ref[pl.ds(rn, ATTN_BLOCK), :], preferred_element_type=F32)
             + jnp.dot(p_c.astype(BF16), cvb_ref[...], preferred_element_type=F32))
        o = o / denom
        for g in range(KV_GROUP):
            o_ref[pl.ds(r0, ATTN_BLOCK), g * HEAD_DIM:(g + 1) * HEAD_DIM] = (
                o[g * ATTN_BLOCK:(g + 1) * ATTN_BLOCK, :].astype(o_ref.dtype))
        return carry

    lax.fori_loop(0, n_blocks, body, 0)


def _rope_tables():
    pos = jnp.arange(LAT_LEN, dtype=jnp.int32)
    row = (pos // GRID_W).astype(F32)
    col = (pos % GRID_W).astype(F32)
    half = HEAD_DIM // 2
    inv_freq = ROPE_BASE ** (-jnp.arange(0, half, 2, dtype=F32) / half)
    ang_r = row[:, None] * inv_freq[None, :]
    ang_c = col[:, None] * inv_freq[None, :]
    zero = jnp.zeros_like(ang_r)
    cos = jnp.concatenate([jnp.cos(ang_r), jnp.cos(ang_r), jnp.cos(ang_c), jnp.cos(ang_c)], axis=1)
    sin_lo = jnp.concatenate([-jnp.sin(ang_r), zero, -jnp.sin(ang_c), zero], axis=1)
    sin_hi = jnp.concatenate([zero, jnp.sin(ang_r), zero, jnp.sin(ang_c)], axis=1)
    return cos, sin_lo, sin_hi


def _attention(qkv, cache_k, cache_v, sink, layer_a):
    q_cols = KV_GROUP * HEAD_DIM
    k_blk0 = N_HEADS
    v_blk0 = N_HEADS + N_KV_HEADS
    ctx_rows0 = T_CTX // LAT_LEN
    cos, sin_lo, sin_hi = _rope_tables()
    ck = cache_k.reshape(N_LAT_REQ, -1, PAST_LEN, N_KV_HEADS * HEAD_DIM)
    cv = cache_v.reshape(N_LAT_REQ, -1, PAST_LEN, N_KV_HEADS * HEAD_DIM)
    tab_spec = pl.BlockSpec((LAT_LEN, HEAD_DIM), lambda b, h: (0, 0))
    smem_spec = pl.BlockSpec(memory_space=pltpu.SMEM)
    lat_block = LAT_LEN * HEAD_DIM * 4
    vmem_lat = (2 * (q_cols // HEAD_DIM + 2 + 1 + 3) * lat_block + 2 * LAT_LEN * q_cols * 2
                + 3 * lat_block + 16 * MIB)
    o_lat = pl.pallas_call(
        _attn_lat_kernel,
        grid=(N_LAT_REQ, N_KV_HEADS),
        in_specs=[
            smem_spec,
            pl.BlockSpec((LAT_LEN, q_cols), lambda b, h: (ctx_rows0 + b, h)),
            pl.BlockSpec((LAT_LEN, HEAD_DIM), lambda b, h: (ctx_rows0 + b, k_blk0 + h)),
            pl.BlockSpec((LAT_LEN, HEAD_DIM), lambda b, h: (ctx_rows0 + b, v_blk0 + h)),
            pl.BlockSpec((None, None, PAST_LEN, HEAD_DIM), lambda b, h: (b, layer_a, 0, h)),
            pl.BlockSpec((None, None, PAST_LEN, HEAD_DIM), lambda b, h: (b, layer_a, 0, h)),
            tab_spec, tab_spec, tab_spec,
        ],
        out_specs=pl.BlockSpec((LAT_LEN, q_cols), lambda b, h: (ctx_rows0 + b, h)),
        out_shape=jax.ShapeDtypeStruct((T_ALL, N_HEADS * HEAD_DIM), BF16),
        scratch_shapes=[pltpu.VMEM((LAT_LEN, HEAD_DIM), BF16), pltpu.VMEM((LAT_LEN, HEAD_DIM), BF16),
                        pltpu.VMEM((PAST_LEN, HEAD_DIM), BF16), pltpu.VMEM((PAST_LEN, HEAD_DIM), BF16)],
        compiler_params=_params(vmem_lat, 2),
        name="attn_latent",
    )(sink, qkv, qkv, qkv, ck, cv, cos, sin_lo, sin_hi)

    vmem_ctx = 2 * (CTX_LEN * q_cols * 4 + 2 * CTX_LEN * HEAD_DIM * 4 + CTX_LEN * q_cols * 2) + 8 * MIB
    return pl.pallas_call(
        _attn_ctx_kernel,
        grid=(N_CTX_REQ, N_KV_HEADS),
        in_specs=[
            smem_spec,
            pl.BlockSpec((CTX_LEN, q_cols), lambda b, h: (b, h)),
            pl.BlockSpec((CTX_LEN, HEAD_DIM), lambda b, h: (b, k_blk0 + h)),
            pl.BlockSpec((CTX_LEN, HEAD_DIM), lambda b, h: (b, v_blk0 + h)),
            pl.BlockSpec(memory_space=pl.ANY),
        ],
        out_specs=pl.BlockSpec((CTX_LEN, q_cols), lambda b, h: (b, h)),
        out_shape=jax.ShapeDtypeStruct((T_ALL, N_HEADS * HEAD_DIM), BF16),
        input_output_aliases={4: 0},
        compiler_params=_params(vmem_ctx, 2),
        name="attn_context",
    )(sink, qkv, qkv, qkv, o_lat)


def _dft_tables(n):
    k = jnp.arange(n, dtype=jnp.int32)
    ang = ((k[:, None] * k[None, :]) % n).astype(F32) * (2.0 * math.pi / n)
    return jnp.cos(ang), jnp.sin(ang)


def _fourier_group(x, cs_ref, pos_ref, norm):
    z = jnp.dot(x, cs_ref[...], preferred_element_type=F32)
    zz = jnp.concatenate([z[:, :GROUP_DIM], z[:, GROUP_DIM:]], axis=0).astype(BF16)
    return jnp.dot(pos_ref[...], zz, preferred_element_type=F32) * norm


def _fourier_lat_kernel(x_ref, cs_ref, pos_ref, o_ref):
    norm = (LAT_LEN * GROUP_DIM) ** -0.5
    o_ref[...] = _fourier_group(x_ref[...], cs_ref, pos_ref, norm).astype(o_ref.dtype)


def _fourier_ctx_kernel(x_ref, cs_ref, pos_ref, prev_ref, o_ref):
    del prev_ref
    norm = (CTX_LEN * GROUP_DIM) ** -0.5
    for g in range(N_GROUPS):
        cols = slice(g * GROUP_DIM, (g + 1) * GROUP_DIM)
        o_ref[:, cols] = _fourier_group(x_ref[:, cols], cs_ref, pos_ref, norm).astype(o_ref.dtype)


def _fourier(h):
    cc, sc = _dft_tables(GROUP_DIM)
    cs = jnp.concatenate([cc, sc], axis=1).astype(BF16)
    cl, sl = _dft_tables(LAT_LEN)
    pos_lat = jnp.concatenate([cl, -sl], axis=1).astype(BF16)
    cp, sp = _dft_tables(CTX_LEN)
    pos_ctx = jnp.concatenate([cp, -sp], axis=1).astype(BF16)
    ctx_rows0 = T_CTX // LAT_LEN

    blk = LAT_LEN * GROUP_DIM
    vmem_lat = 2 * (2 * blk * 2 + GROUP_DIM * 2 * GROUP_DIM * 2 + LAT_LEN * 2 * LAT_LEN * 2) + 5 * blk * 4 + 4 * MIB
    f_lat = pl.pallas_call(
        _fourier_lat_kernel,
        grid=(N_LAT_REQ, N_GROUPS),
        in_specs=[pl.BlockSpec((LAT_LEN, GROUP_DIM), lambda b, g: (ctx_rows0 + b, g)),
                  pl.BlockSpec((GROUP_DIM, 2 * GROUP_DIM), lambda b, g: (0, 0)),
                  pl.BlockSpec((LAT_LEN, 2 * LAT_LEN), lambda b, g: (0, 0))],
        out_specs=pl.BlockSpec((LAT_LEN, GROUP_DIM), lambda b, g: (ctx_rows0 + b, g)),
        out_shape=jax.ShapeDtypeStruct((T_ALL, D_MODEL), BF16),
        compiler_params=_params(vmem_lat, 2),
        name="fourier_latent",
    )(h, cs, pos_lat)

    vmem_ctx = 2 * (2 * CTX_LEN * D_MODEL * 2 + GROUP_DIM * 2 * GROUP_DIM * 2 + CTX_LEN * 2 * CTX_LEN * 2) + 16 * MIB
    return pl.pallas_call(
        _fourier_ctx_kernel,
        grid=(N_CTX_REQ,),
        in_specs=[pl.BlockSpec((CTX_LEN, D_MODEL), lambda b: (b, 0)),
                  pl.BlockSpec((GROUP_DIM, 2 * GROUP_DIM), lambda b: (0, 0)),
                  pl.BlockSpec((CTX_LEN, 2 * CTX_LEN), lambda b: (0, 0)),
                  pl.BlockSpec(memory_space=pl.ANY)],
        out_specs=pl.BlockSpec((CTX_LEN, D_MODEL), lambda b: (b, 0)),
        out_shape=jax.ShapeDtypeStruct((T_ALL, D_MODEL), BF16),
        input_output_aliases={3: 0},
        compiler_params=_params(vmem_ctx, 1),
        name="fourier_context",
    )(h, cs, pos_ctx, f_lat)


def kernel(x_prompt, x_sample, cache_k, cache_v, c, c_ctx, w_ada, b_ada, g_mix_pre, g_mix_post,
           g_ffn_pre, g_ffn_post, w_qkv, w_attn_out, attn_sink, w_fourier, w_ffn_in, w_ffn_out):
    y = jnp.concatenate([x_prompt.reshape(T_CTX, D_MODEL), x_sample.reshape(T_LAT, D_MODEL)], axis=0)
    cond = jnp.concatenate(
        [c, c_ctx[None, :], jnp.zeros((MOD_ROWS - N_LAT_REQ - 1, D_MODEL), F32)], axis=0)
    mod = _ada_table(cond, w_ada, b_ada).reshape(DEPTH, MOD_ROWS, 6, D_MODEL)
    g_mix_pre, g_mix_post, g_ffn_pre, g_ffn_post = (
        g.reshape(DEPTH, 1, D_MODEL) for g in (g_mix_pre, g_mix_post, g_ffn_pre, g_ffn_post))

    new_k, new_v = [], []
    h = _pre_call(y, g_mix_pre, mod, 0, 0)
    for i in range(DEPTH):
        if i % 2 == 0:
            a = i // 2
            qkv = _matmul(h, w_qkv[a].astype(BF16), rows=MM_ROWS, cols=1024, out_dtype=F32, name="qkv_proj")
            k0 = N_HEADS * HEAD_DIM
            k1 = k0 + N_KV_HEADS * HEAD_DIM
            new_k.append(qkv[:T_CTX, k0:k1].reshape(N_CTX_REQ, CTX_LEN, N_KV_HEADS, HEAD_DIM))
            new_v.append(qkv[:T_CTX, k1:].reshape(N_CTX_REQ, CTX_LEN, N_KV_HEADS, HEAD_DIM))
            att = _attention(qkv, cache_k, cache_v, attn_sink[a], a)
            o = _matmul(att, w_attn_out[a].astype(BF16), rows=MM_ROWS, cols=1024, out_dtype=F32,
                        name="attn_out_proj")
        else:
            f = i // 2
            o = _matmul(_fourier(h), w_fourier[f].astype(BF16), rows=MM_ROWS, cols=1024, out_dtype=F32,
                        name="fourier_proj")
        y, h = _postpre_call(o, y, g_mix_post, mod, i, 0, g_ffn_pre, i, 1)
        act = _swiglu_in(h, w_ffn_in[i].astype(BF16))
        o = _matmul(act, w_ffn_out[i].astype(BF16), rows=FFN_OUT_ROWS, cols=512, out_dtype=F32,
                    name="ffn_out_proj")
        if i + 1 < DEPTH:
            y, h = _postpre_call(o, y, g_ffn_post, mod, i, 1, g_mix_pre, i + 1, 0)
        else:
            y = _post_call(o, y, g_ffn_post, mod, i, 1)

    y_p = y[:T_CTX].reshape(N_CTX_REQ, CTX_LEN, D_MODEL)
    y_s = y[T_CTX:].reshape(N_LAT_REQ, LAT_LEN, D_MODEL)
    return (y_p, y_s, jnp.stack(new_k, axis=1), jnp.stack(new_v, axis=1))
```

```python
import functools
import math

import jax
import jax.numpy as jnp
from jax import lax
from jax.experimental import pallas as pl
from jax.experimental.pallas import tpu as pltpu

F32 = jnp.float32
BF16 = jnp.bfloat16

D_MODEL = 4096
N_CTX_REQ = 16
CTX_LEN = 256
N_LAT_REQ = 8
LAT_LEN = 1024
PAST_LEN = 512
DEPTH = 4
GRID_W = 64
HEAD_DIM = 128
N_HEADS = 32
N_KV_HEADS = 8
KV_GROUP = 4
ATTN_BLOCK = 128
ATTN_GROUP = 4
ROPE_BASE = 10000.0
N_GROUPS = 8
GROUP_DIM = D_MODEL // N_GROUPS
D_FF = 11008
EPS = 1e-6
NEG = -1e30

T_CTX = N_CTX_REQ * CTX_LEN
T_LAT = N_LAT_REQ * LAT_LEN
T_ALL = T_CTX + T_LAT
MOD_ROWS = 16
CTX_MOD_ROW = N_LAT_REQ
QKV_W = (N_HEADS + 2 * N_KV_HEADS) * HEAD_DIM
ATTN_SCALE = HEAD_DIM ** -0.5

V7X_VMEM_BYTES = 64 * 1024 * 1024
MIB = 1024 * 1024

ELEM_ROWS = 256
MM_ROWS = 1024
FFN_OUT_ROWS = 512
FFN_TILE = 256


def _params(vmem_bytes, ngrid):
    assert vmem_bytes <= V7X_VMEM_BYTES
    return pltpu.CompilerParams(
        dimension_semantics=("arbitrary",) * ngrid,
        vmem_limit_bytes=int(vmem_bytes),
    )


def _mod_row_of_block(i, rows_per_block):
    ctx_blocks = T_CTX // rows_per_block
    blocks_per_lat = LAT_LEN // rows_per_block
    return jnp.where(i < ctx_blocks, CTX_MOD_ROW, (i - ctx_blocks) // blocks_per_lat)


ADA_TILE = 512


def _ada_kernel(c_ref, w_ref, b_ref, o_ref):
    c = c_ref[...]
    s = (c * jax.nn.sigmoid(c)).astype(BF16)
    w = w_ref[0].astype(BF16)
    o_ref[0] = jnp.dot(s, w, preferred_element_type=F32) + b_ref[0]


def _ada_table(cond, w_ada, b_ada):
    n_out = 6 * D_MODEL
    vmem = 2 * (D_MODEL * ADA_TILE * 4) + D_MODEL * ADA_TILE * 2 + 4 * MIB
    return pl.pallas_call(
        _ada_kernel,
        grid=(DEPTH, n_out // ADA_TILE),
        in_specs=[
            pl.BlockSpec((MOD_ROWS, D_MODEL), lambda l, j: (0, 0)),
            pl.BlockSpec((1, D_MODEL, ADA_TILE), lambda l, j: (l, 0, j)),
            pl.BlockSpec((1, 1, ADA_TILE), lambda l, j: (l, 0, j)),
        ],
        out_specs=pl.BlockSpec((1, MOD_ROWS, ADA_TILE), lambda l, j: (l, 0, j)),
        out_shape=jax.ShapeDtypeStruct((DEPTH, MOD_ROWS, n_out), F32),
        compiler_params=_params(vmem, 2),
        name="ada_table",
    )(cond, w_ada, b_ada.reshape(DEPTH, 1, n_out))


def _rms(x, g):
    return x * lax.rsqrt(jnp.mean(x * x, axis=-1, keepdims=True) + EPS) * g


def _pre(y, g_pre, mod_ref, sub):
    shift = mod_ref[pl.ds(3 * sub, 1), :]
    scale = mod_ref[pl.ds(3 * sub + 1, 1), :]
    return _rms(y, g_pre) * (1.0 + scale) + shift


_CTX_BLOCKS = T_CTX // ELEM_ROWS


def _read_stream(yp_ref, ys_ref):
    return jnp.where(pl.program_id(0) < _CTX_BLOCKS, yp_ref[...], ys_ref[...])


def _pre_kernel(sub, yp_ref, ys_ref, gpre_ref, mod_ref, h_ref):
    h_ref[...] = _pre(_read_stream(yp_ref, ys_ref), gpre_ref[...], mod_ref, sub).astype(h_ref.dtype)


def _post_kernel(sub, o_ref, y_ref, gpost_ref, mod_ref, yp_ref, ys_ref):
    gate = mod_ref[pl.ds(3 * sub + 2, 1), :]
    yn = y_ref[...] + gate * _rms(o_ref[...], gpost_ref[...])

    @pl.when(pl.program_id(0) < _CTX_BLOCKS)
    def _():
        yp_ref[...] = yn

    @pl.when(pl.program_id(0) >= _CTX_BLOCKS)
    def _():
        ys_ref[...] = yn


def _postpre_kernel(sub, nsub, split_in, o_ref, *refs):
    if split_in:
        yp_ref, ys_ref, gpost_ref, mod_ref, gpre_ref, nmod_ref, yo_ref, h_ref = refs
        y = _read_stream(yp_ref, ys_ref)
    else:
        y_ref, gpost_ref, mod_ref, gpre_ref, nmod_ref, yo_ref, h_ref = refs
        y = y_ref[...]
    gate = mod_ref[pl.ds(3 * sub + 2, 1), :]
    yn = y + gate * _rms(o_ref[...], gpost_ref[...])
    yo_ref[...] = yn
    h_ref[...] = _pre(yn, gpre_ref[...], nmod_ref, nsub).astype(h_ref.dtype)


def _row_spec():
    return pl.BlockSpec((ELEM_ROWS, D_MODEL), lambda i: (i, 0))


def _split_specs():
    return [pl.BlockSpec((ELEM_ROWS, D_MODEL), lambda i: (jnp.minimum(i, _CTX_BLOCKS - 1), 0)),
            pl.BlockSpec((ELEM_ROWS, D_MODEL), lambda i: (jnp.maximum(i - _CTX_BLOCKS, 0), 0))]


def _gain_spec(layer):
    return pl.BlockSpec((None, 1, D_MODEL), lambda i: (layer, 0, 0))


def _mod_spec(layer):
    return pl.BlockSpec((None, None, 6, D_MODEL),
                        lambda i: (layer, _mod_row_of_block(i, ELEM_ROWS), 0, 0))


_ELEM_BLOCK_F32 = ELEM_ROWS * D_MODEL * 4


def _pre_call(xp, xs, g_pre, mod, layer, sub):
    vmem = 2 * (2 * _ELEM_BLOCK_F32 + _ELEM_BLOCK_F32 // 2) + 3 * _ELEM_BLOCK_F32 + 2 * MIB
    return pl.pallas_call(
        functools.partial(_pre_kernel, sub),
        grid=(T_ALL // ELEM_ROWS,),
        in_specs=_split_specs() + [_gain_spec(layer), _mod_spec(layer)],
        out_specs=_row_spec(),
        out_shape=jax.ShapeDtypeStruct((T_ALL, D_MODEL), BF16),
        compiler_params=_params(vmem, 1),
        name="pre_norm",
    )(xp, xs, g_pre, mod)


def _post_call(o, y, g_post, mod, layer, sub):
    vmem = 2 * 4 * _ELEM_BLOCK_F32 + 3 * _ELEM_BLOCK_F32 + 2 * MIB
    return pl.pallas_call(
        functools.partial(_post_kernel, sub),
        grid=(T_ALL // ELEM_ROWS,),
        in_specs=[_row_spec(), _row_spec(), _gain_spec(layer), _mod_spec(layer)],
        out_specs=_split_specs(),
        out_shape=[jax.ShapeDtypeStruct((T_CTX, D_MODEL), F32),
                   jax.ShapeDtypeStruct((T_LAT, D_MODEL), F32)],
        compiler_params=_params(vmem, 1),
        name="post_norm",
    )(o, y, g_post, mod)


def _postpre_call(o, y, g_post, mod, layer, sub, g_pre, nlayer, nsub):
    split_in = isinstance(y, tuple)
    y_args = y if split_in else (y,)
    y_specs = _split_specs() if split_in else [_row_spec()]
    vmem = 2 * ((2 + len(y_args)) * _ELEM_BLOCK_F32 + _ELEM_BLOCK_F32 // 2) + 4 * _ELEM_BLOCK_F32 + 2 * MIB
    return pl.pallas_call(
        functools.partial(_postpre_kernel, sub, nsub, split_in),
        grid=(T_ALL // ELEM_ROWS,),
        in_specs=[_row_spec()] + y_specs + [_gain_spec(layer), _mod_spec(layer),
                                            _gain_spec(nlayer), _mod_spec(nlayer)],
        out_specs=[_row_spec(), _row_spec()],
        out_shape=[jax.ShapeDtypeStruct((T_ALL, D_MODEL), F32),
                   jax.ShapeDtypeStruct((T_ALL, D_MODEL), BF16)],
        compiler_params=_params(vmem, 1),
        name="post_pre_norm",
    )(o, *y_args, g_post, mod, g_pre, mod)


def _mm_kernel(x_ref, w_ref, o_ref):
    o_ref[...] = jnp.dot(x_ref[...], w_ref[...], preferred_element_type=F32).astype(o_ref.dtype)


def _matmul_rows_outer(x, w_stack, layer, *, rows, cols, out_dtype, name):
    m, k = x.shape
    n = w_stack.shape[2]
    out_bytes = jnp.dtype(out_dtype).itemsize
    vmem = 2 * (rows * k * 2 + k * cols * 2 + rows * cols * out_bytes) + 2 * rows * cols * 4 + 2 * MIB
    return pl.pallas_call(
        _mm_kernel,
        grid=(m // rows, n // cols),
        in_specs=[pl.BlockSpec((rows, k), lambda i, j: (i, 0)),
                  pl.BlockSpec((None, k, cols), lambda i, j: (layer, 0, j))],
        out_specs=pl.BlockSpec((rows, cols), lambda i, j: (i, j)),
        out_shape=jax.ShapeDtypeStruct((m, n), out_dtype),
        compiler_params=_params(vmem, 2),
        name=name,
    )(x, w_stack)


def _mm_wcast_kernel(x_ref, w_ref, o_ref, wb_ref):
    @pl.when(pl.program_id(1) == 0)
    def _():
        wb_ref[...] = w_ref[...].astype(BF16)

    o_ref[...] = jnp.dot(x_ref[...], wb_ref[...], preferred_element_type=F32).astype(o_ref.dtype)


def _matmul_cols_outer(x, w_stack, layer, *, cols, out_dtype, name):
    m, k = x.shape
    n = w_stack.shape[2]
    rows = MM_ROWS
    out_bytes = jnp.dtype(out_dtype).itemsize
    vmem = (2 * (rows * k * 2 + k * cols * 4 + rows * cols * out_bytes) + k * cols * 2
            + 2 * rows * cols * 4 + 2 * MIB)
    return pl.pallas_call(
        _mm_wcast_kernel,
        grid=(n // cols, m // rows),
        in_specs=[pl.BlockSpec((rows, k), lambda j, i: (i, 0)),
                  pl.BlockSpec((None, k, cols), lambda j, i: (layer, 0, j))],
        out_specs=pl.BlockSpec((rows, cols), lambda j, i: (i, j)),
        out_shape=jax.ShapeDtypeStruct((m, n), out_dtype),
        scratch_shapes=[pltpu.VMEM((k, cols), BF16)],
        compiler_params=_params(vmem, 2),
        name=name,
    )(x, w_stack)


def _swiglu_in_kernel(x_ref, wg_ref, wu_ref, o_ref, wb_ref):
    @pl.when(pl.program_id(1) == 0)
    def _():
        wb_ref[:, :FFN_TILE] = wg_ref[...].astype(BF16)
        wb_ref[:, FFN_TILE:] = wu_ref[...].astype(BF16)

    gu = jnp.dot(x_ref[...], wb_ref[...], preferred_element_type=F32)
    g = gu[:, :FFN_TILE]
    u = gu[:, FFN_TILE:]
    o_ref[...] = (g * jax.nn.sigmoid(g) * u).astype(o_ref.dtype)


def _swiglu_in(h, w_in_stack, layer):
    n_tiles = D_FF // FFN_TILE
    vmem = (2 * (MM_ROWS * D_MODEL * 2 + 2 * D_MODEL * FFN_TILE * 4 + MM_ROWS * FFN_TILE * 2)
            + D_MODEL * 2 * FFN_TILE * 2 + 4 * MM_ROWS * FFN_TILE * 4 + 2 * MIB)
    return pl.pallas_call(
        _swiglu_in_kernel,
        grid=(n_tiles, T_ALL // MM_ROWS),
        in_specs=[pl.BlockSpec((MM_ROWS, D_MODEL), lambda j, i: (i, 0)),
                  pl.BlockSpec((None, D_MODEL, FFN_TILE), lambda j, i: (layer, 0, j)),
                  pl.BlockSpec((None, D_MODEL, FFN_TILE), lambda j, i: (layer, 0, n_tiles + j))],
        out_specs=pl.BlockSpec((MM_ROWS, FFN_TILE), lambda j, i: (i, j)),
        out_shape=jax.ShapeDtypeStruct((T_ALL, D_FF), BF16),
        scratch_shapes=[pltpu.VMEM((D_MODEL, 2 * FFN_TILE), BF16)],
        compiler_params=_params(vmem, 2),
        name="ffn_in_swiglu",
    )(h, w_in_stack, w_in_stack)


def _attn_ctx_kernel(sink_ref, q_ref, k_ref, v_ref, prev_ref, o_ref):
    del prev_ref
    kh = pl.program_id(1)
    kt = k_ref[...].T.astype(BF16)
    v = v_ref[...].astype(BF16)
    for g in range(KV_GROUP):
        cols = slice(g * HEAD_DIM, (g + 1) * HEAD_DIM)
        q = (q_ref[:, cols] * ATTN_SCALE).astype(BF16)
        s = jnp.dot(q, kt, preferred_element_type=F32)
        sk = sink_ref[kh * KV_GROUP + g]
        m = jnp.maximum(jnp.max(_fold_lanes(s, jnp.maximum), axis=-1, keepdims=True), sk)
        p = jnp.exp(s - m)
        denom = jnp.sum(_fold_lanes(p, jnp.add), axis=-1, keepdims=True) + jnp.exp(sk - m)
        o = jnp.dot(p.astype(BF16), v, preferred_element_type=F32)
        o_ref[:, cols] = (o / denom).astype(o_ref.dtype)


def _fold_lanes(x, op):
    tiles = [x[:, t * HEAD_DIM:(t + 1) * HEAD_DIM] for t in range(x.shape[1] // HEAD_DIM)]
    while len(tiles) > 1:
        tiles = [op(tiles[t], tiles[t + 1]) for t in range(0, len(tiles), 2)]
    return tiles[0]


def _rope(x, cos, sin_lo, sin_hi):
    return x * cos + pltpu.roll(x, 96, 1) * sin_lo + pltpu.roll(x, 32, 1) * sin_hi


def _attn_lat_kernel(sink_ref, q_ref, k_ref, v_ref, ck_ref, cv_ref, cos_ref, slo_ref, shi_ref,
                     o_ref, kt_ref, vb_ref, ckt_ref, cvb_ref):
    kh = pl.program_id(1)
    n_blocks = LAT_LEN // ATTN_BLOCK
    for n in range(n_blocks):
        blk = slice(n * ATTN_BLOCK, (n + 1) * ATTN_BLOCK)
        kt_ref[n] = _rope(k_ref[blk, :], cos_ref[blk, :], slo_ref[blk, :], shi_ref[blk, :]).T.astype(BF16)
    vb_ref[...] = v_ref[...].astype(BF16)
    ckt_ref[...] = ck_ref[...].T.astype(BF16)
    cvb_ref[...] = cv_ref[...].astype(BF16)

    rows = KV_GROUP * ATTN_BLOCK
    qi = lax.broadcasted_iota(jnp.int32, (rows, ATTN_BLOCK), 0) % ATTN_BLOCK
    kj = lax.broadcasted_iota(jnp.int32, (rows, ATTN_BLOCK), 1)
    band_prev = kj >= qi
    band_next = kj <= qi
    sk = jnp.concatenate(
        [jnp.full((ATTN_BLOCK, 1), sink_ref[kh * KV_GROUP + g], F32) for g in range(KV_GROUP)], axis=0)

    def scores(n):
        r0 = pl.multiple_of(n * ATTN_BLOCK, ATTN_BLOCK)
        cos = cos_ref[pl.ds(r0, ATTN_BLOCK), :]
        slo = slo_ref[pl.ds(r0, ATTN_BLOCK), :]
        shi = shi_ref[pl.ds(r0, ATTN_BLOCK), :]
        q = jnp.concatenate(
            [(_rope(q_ref[pl.ds(r0, ATTN_BLOCK), g * HEAD_DIM:(g + 1) * HEAD_DIM], cos, slo, shi)
              * ATTN_SCALE).astype(BF16) for g in range(KV_GROUP)], axis=0)
        n_prev = jnp.maximum(n - 1, 0)
        n_next = jnp.minimum(n + 1, n_blocks - 1)
        s_p = jnp.where(band_prev & (n >= 1),
                        jnp.dot(q, kt_ref[n_prev], preferred_element_type=F32), NEG)
        s_0 = jnp.dot(q, kt_ref[n], preferred_element_type=F32)
        s_n = jnp.where(band_next & (n <= n_blocks - 2),
                        jnp.dot(q, kt_ref[n_next], preferred_element_type=F32), NEG)
        s_c = jnp.dot(q, ckt_ref[...], preferred_element_type=F32)
        return s_p, s_0, s_n, s_c

    def softmax(s_p, s_0, s_n, s_c):
        m = jnp.maximum(jnp.maximum(s_p, s_0), jnp.maximum(s_n, _fold_lanes(s_c, jnp.maximum)))
        m = jnp.maximum(jnp.max(m, axis=-1, keepdims=True), sk)
        p_p = jnp.exp(s_p - m)
        p_0 = jnp.exp(s_0 - m)
        p_n = jnp.exp(s_n - m)
        p_c = jnp.exp(s_c - m)
        denom = (jnp.sum((p_p + p_0) + (p_n + _fold_lanes(p_c, jnp.add)), axis=-1, keepdims=True)
                 + jnp.exp(sk - m))
        return p_p.astype(BF16), p_0.astype(BF16), p_n.astype(BF16), p_c.astype(BF16), denom

    def weighted_values(n, p_p, p_0, p_n, p_c, denom):
        r0 = pl.multiple_of(n * ATTN_BLOCK, ATTN_BLOCK)
        rp = pl.multiple_of(jnp.maximum(n - 1, 0) * ATTN_BLOCK, ATTN_BLOCK)
        rn = pl.multiple_of(jnp.minimum(n + 1, n_blocks - 1) * ATTN_BLOCK, ATTN_BLOCK)
        o = (jnp.dot(p_p, vb_ref[pl.ds(rp, ATTN_BLOCK), :], preferred_element_type=F32)
             + jnp.dot(p_0, vb_ref[pl.ds(r0, ATTN_BLOCK), :], preferred_element_type=F32)
             + jnp.dot(p_n, vb_ref[pl.ds(rn, ATTN_BLOCK), :], preferred_element_type=F32)
             + jnp.dot(p_c, cvb_ref[...], preferred_element_type=F32))
        o = o / denom
        for g in range(KV_GROUP):
            o_ref[pl.ds(r0, ATTN_BLOCK), g * HEAD_DIM:(g + 1) * HEAD_DIM] = (
                o[g * ATTN_BLOCK:(g + 1) * ATTN_BLOCK, :].astype(o_ref.dtype))

    def group(i, carry):
        blocks = [i * ATTN_GROUP + j for j in range(ATTN_GROUP)]
        s = [scores(n) for n in blocks]
        p = [softmax(*sn) for sn in s]
        for n, pn in zip(blocks, p):
            weighted_values(n, *pn)
        return carry

    lax.fori_loop(0, n_blocks // ATTN_GROUP, group, 0)


def _rope_tables():
    pos = jnp.arange(LAT_LEN, dtype=jnp.int32)
    row = (pos // GRID_W).astype(F32)
    col = (pos % GRID_W).astype(F32)
    half = HEAD_DIM // 2
    inv_freq = ROPE_BASE ** (-jnp.arange(0, half, 2, dtype=F32) / half)
    ang_r = row[:, None] * inv_freq[None, :]
    ang_c = col[:, None] * inv_freq[None, :]
    zero = jnp.zeros_like(ang_r)
    cos = jnp.concatenate([jnp.cos(ang_r), jnp.cos(ang_r), jnp.cos(ang_c), jnp.cos(ang_c)], axis=1)
    sin_lo = jnp.concatenate([-jnp.sin(ang_r), zero, -jnp.sin(ang_c), zero], axis=1)
    sin_hi = jnp.concatenate([zero, jnp.sin(ang_r), zero, jnp.sin(ang_c)], axis=1)
    return cos, sin_lo, sin_hi


def _attention(qkv, cache_k, cache_v, sink, layer_a):
    q_cols = KV_GROUP * HEAD_DIM
    k_blk0 = N_HEADS
    v_blk0 = N_HEADS + N_KV_HEADS
    ctx_rows0 = T_CTX // LAT_LEN
    cos, sin_lo, sin_hi = _rope_tables()
    ck = cache_k.reshape(N_LAT_REQ, -1, PAST_LEN, N_KV_HEADS * HEAD_DIM)
    cv = cache_v.reshape(N_LAT_REQ, -1, PAST_LEN, N_KV_HEADS * HEAD_DIM)
    tab_spec = pl.BlockSpec((LAT_LEN, HEAD_DIM), lambda b, h: (0, 0))
    smem_spec = pl.BlockSpec(memory_space=pltpu.SMEM)
    lat_block = LAT_LEN * HEAD_DIM * 4
    vmem_lat = (2 * (q_cols // HEAD_DIM + 2 + 1 + 3) * lat_block + 2 * LAT_LEN * q_cols * 2
                + 3 * lat_block + 16 * MIB)
    o_lat = pl.pallas_call(
        _attn_lat_kernel,
        grid=(N_LAT_REQ, N_KV_HEADS),
        in_specs=[
            smem_spec,
            pl.BlockSpec((LAT_LEN, q_cols), lambda b, h: (ctx_rows0 + b, h)),
            pl.BlockSpec((LAT_LEN, HEAD_DIM), lambda b, h: (ctx_rows0 + b, k_blk0 + h)),
            pl.BlockSpec((LAT_LEN, HEAD_DIM), lambda b, h: (ctx_rows0 + b, v_blk0 + h)),
            pl.BlockSpec((None, None, PAST_LEN, HEAD_DIM), lambda b, h: (b, layer_a, 0, h)),
            pl.BlockSpec((None, None, PAST_LEN, HEAD_DIM), lambda b, h: (b, layer_a, 0, h)),
            tab_spec, tab_spec, tab_spec,
        ],
        out_specs=pl.BlockSpec((LAT_LEN, q_cols), lambda b, h: (ctx_rows0 + b, h)),
        out_shape=jax.ShapeDtypeStruct((T_ALL, N_HEADS * HEAD_DIM), BF16),
        scratch_shapes=[pltpu.VMEM((LAT_LEN // ATTN_BLOCK, HEAD_DIM, ATTN_BLOCK), BF16),
                        pltpu.VMEM((LAT_LEN, HEAD_DIM), BF16),
                        pltpu.VMEM((HEAD_DIM, PAST_LEN), BF16), pltpu.VMEM((PAST_LEN, HEAD_DIM), BF16)],
        compiler_params=_params(vmem_lat, 2),
        name="attn_latent",
    )(sink, qkv, qkv, qkv, ck, cv, cos, sin_lo, sin_hi)

    vmem_ctx = 2 * (CTX_LEN * q_cols * 4 + 2 * CTX_LEN * HEAD_DIM * 4 + CTX_LEN * q_cols * 2) + 8 * MIB
    return pl.pallas_call(
        _attn_ctx_kernel,
        grid=(N_CTX_REQ, N_KV_HEADS),
        in_specs=[
            smem_spec,
            pl.BlockSpec((CTX_LEN, q_cols), lambda b, h: (b, h)),
            pl.BlockSpec((CTX_LEN, HEAD_DIM), lambda b, h: (b, k_blk0 + h)),
            pl.BlockSpec((CTX_LEN, HEAD_DIM), lambda b, h: (b, v_blk0 + h)),
            pl.BlockSpec(memory_space=pl.ANY),
        ],
        out_specs=pl.BlockSpec((CTX_LEN, q_cols), lambda b, h: (b, h)),
        out_shape=jax.ShapeDtypeStruct((T_ALL, N_HEADS * HEAD_DIM), BF16),
        input_output_aliases={4: 0},
        compiler_params=_params(vmem_ctx, 2),
        name="attn_context",
    )(sink, qkv, qkv, qkv, o_lat)


def _dft_tables(n):
    k = jnp.arange(n, dtype=jnp.int32)
    ang = ((k[:, None] * k[None, :]) % n).astype(F32) * (2.0 * math.pi / n)
    return jnp.cos(ang), jnp.sin(ang)


def _fourier_group(x, cs_ref, pos_ref, norm):
    z = jnp.dot(x, cs_ref[...], preferred_element_type=F32)
    zz = jnp.concatenate([z[:, :GROUP_DIM], z[:, GROUP_DIM:]], axis=0).astype(BF16)
    return jnp.dot(pos_ref[...], zz, preferred_element_type=F32) * norm


def _fourier_lat_kernel(x_ref, cs_ref, pos_ref, o_ref):
    norm = (LAT_LEN * GROUP_DIM) ** -0.5
    o_ref[...] = _fourier_group(x_ref[...], cs_ref, pos_ref, norm).astype(o_ref.dtype)


def _fourier_ctx_kernel(x_ref, cs_ref, pos_ref, prev_ref, o_ref):
    del prev_ref
    norm = (CTX_LEN * GROUP_DIM) ** -0.5
    for g in range(N_GROUPS):
        cols = slice(g * GROUP_DIM, (g + 1) * GROUP_DIM)
        o_ref[:, cols] = _fourier_group(x_ref[:, cols], cs_ref, pos_ref, norm).astype(o_ref.dtype)


def _fourier(h):
    cc, sc = _dft_tables(GROUP_DIM)
    cs = jnp.concatenate([cc, sc], axis=1).astype(BF16)
    cl, sl = _dft_tables(LAT_LEN)
    pos_lat = jnp.concatenate([cl, -sl], axis=1).astype(BF16)
    cp, sp = _dft_tables(CTX_LEN)
    pos_ctx = jnp.concatenate([cp, -sp], axis=1).astype(BF16)
    ctx_rows0 = T_CTX // LAT_LEN

    blk = LAT_LEN * GROUP_DIM
    vmem_lat = 2 * (2 * blk * 2 + GROUP_DIM * 2 * GROUP_DIM * 2 + LAT_LEN * 2 * LAT_LEN * 2) + 5 * blk * 4 + 4 * MIB
    f_lat = pl.pallas_call(
        _fourier_lat_kernel,
        grid=(N_LAT_REQ, N_GROUPS),
        in_specs=[pl.BlockSpec((LAT_LEN, GROUP_DIM), lambda b, g: (ctx_rows0 + b, g)),
                  pl.BlockSpec((GROUP_DIM, 2 * GROUP_DIM), lambda b, g: (0, 0)),
                  pl.BlockSpec((LAT_LEN, 2 * LAT_LEN), lambda b, g: (0, 0))],
        out_specs=pl.BlockSpec((LAT_LEN, GROUP_DIM), lambda b, g: (ctx_rows0 + b, g)),
        out_shape=jax.ShapeDtypeStruct((T_ALL, D_MODEL), BF16),
        compiler_params=_params(vmem_lat, 2),
        name="fourier_latent",
    )(h, cs, pos_lat)

    vmem_ctx = 2 * (2 * CTX_LEN * D_MODEL * 2 + GROUP_DIM * 2 * GROUP_DIM * 2 + CTX_LEN * 2 * CTX_LEN * 2) + 16 * MIB
    return pl.pallas_call(
        _fourier_ctx_kernel,
        grid=(N_CTX_REQ,),
        in_specs=[pl.BlockSpec((CTX_LEN, D_MODEL), lambda b: (b, 0)),
                  pl.BlockSpec((GROUP_DIM, 2 * GROUP_DIM), lambda b: (0, 0)),
                  pl.BlockSpec((CTX_LEN, 2 * CTX_LEN), lambda b: (0, 0)),
                  pl.BlockSpec(memory_space=pl.ANY)],
        out_specs=pl.BlockSpec((CTX_LEN, D_MODEL), lambda b: (b, 0)),
        out_shape=jax.ShapeDtypeStruct((T_ALL, D_MODEL), BF16),
        input_output_aliases={3: 0},
        compiler_params=_params(vmem_ctx, 1),
        name="fourier_context",
    )(h, cs, pos_ctx, f_lat)


def kernel(x_prompt, x_sample, cache_k, cache_v, c, c_ctx, w_ada, b_ada, g_mix_pre, g_mix_post,
           g_ffn_pre, g_ffn_post, w_qkv, w_attn_out, attn_sink, w_fourier, w_ffn_in, w_ffn_out):
    xp = x_prompt.reshape(T_CTX, D_MODEL)
    xs = x_sample.reshape(T_LAT, D_MODEL)
    cond = jnp.concatenate(
        [c, c_ctx[None, :], jnp.zeros((MOD_ROWS - N_LAT_REQ - 1, D_MODEL), F32)], axis=0)
    mod = _ada_table(cond, w_ada, b_ada).reshape(DEPTH, MOD_ROWS, 6, D_MODEL)
    g_mix_pre, g_mix_post, g_ffn_pre, g_ffn_post = (
        g.reshape(DEPTH, 1, D_MODEL) for g in (g_mix_pre, g_mix_post, g_ffn_pre, g_ffn_post))
    w_ffn_out_b = w_ffn_out.astype(BF16)

    new_k, new_v = [], []
    y = (xp, xs)
    h = _pre_call(xp, xs, g_mix_pre, mod, 0, 0)
    for i in range(DEPTH):
        if i % 2 == 0:
            a = i // 2
            qkv = _matmul_cols_outer(h, w_qkv, a, cols=512, out_dtype=F32, name="qkv_proj")
            k0 = N_HEADS * HEAD_DIM
            k1 = k0 + N_KV_HEADS * HEAD_DIM
            new_k.append(qkv[:T_CTX, k0:k1].reshape(N_CTX_REQ, CTX_LEN, N_KV_HEADS, HEAD_DIM))
            new_v.append(qkv[:T_CTX, k1:].reshape(N_CTX_REQ, CTX_LEN, N_KV_HEADS, HEAD_DIM))
            att = _attention(qkv, cache_k, cache_v, attn_sink[a], a)
            o = _matmul_cols_outer(att, w_attn_out, a, cols=512, out_dtype=F32, name="attn_out_proj")
        else:
            o = _matmul_cols_outer(_fourier(h), w_fourier, i // 2, cols=512, out_dtype=F32,
                                   name="fourier_proj")
        y, h = _postpre_call(o, y, g_mix_post, mod, i, 0, g_ffn_pre, i, 1)
        act = _swiglu_in(h, w_ffn_in, i)
        o = _matmul_rows_outer(act, w_ffn_out_b, i, rows=FFN_OUT_ROWS, cols=512, out_dtype=F32,
                               name="ffn_out_proj")
        if i + 1 < DEPTH:
            y, h = _postpre_call(o, y, g_ffn_post, mod, i, 1, g_mix_pre, i + 1, 0)
        else:
            y_p, y_s = _post_call(o, y, g_ffn_post, mod, i, 1)

    return (y_p.reshape(N_CTX_REQ, CTX_LEN, D_MODEL), y_s.reshape(N_LAT_REQ, LAT_LEN, D_MODEL),
            jnp.stack(new_k, axis=1), jnp.stack(new_v, axis=1))
```

```python
import functools
import math

import jax
import jax.numpy as jnp
from jax import lax
from jax.experimental import pallas as pl
from jax.experimental.pallas import tpu as pltpu

F32 = jnp.float32
BF16 = jnp.bfloat16

D_MODEL = 4096
N_CTX_REQ = 16
CTX_LEN = 256
N_LAT_REQ = 8
LAT_LEN = 1024
PAST_LEN = 512
DEPTH = 4
GRID_W = 64
HEAD_DIM = 128
N_HEADS = 32
N_KV_HEADS = 8
KV_GROUP = 4
ATTN_BLOCK = 128
ATTN_GROUP = 4
ROPE_BASE = 10000.0
N_GROUPS = 8
GROUP_DIM = D_MODEL // N_GROUPS
D_FF = 11008
EPS = 1e-6
NEG = -1e30

T_CTX = N_CTX_REQ * CTX_LEN
T_LAT = N_LAT_REQ * LAT_LEN
T_ALL = T_CTX + T_LAT
MOD_ROWS = 16
CTX_MOD_ROW = N_LAT_REQ
QKV_W = (N_HEADS + 2 * N_KV_HEADS) * HEAD_DIM
ATTN_SCALE = HEAD_DIM ** -0.5
LOG2E = math.log2(math.e)
Q_SCALE = ATTN_SCALE * LOG2E

V7X_VMEM_BYTES = 64 * 1024 * 1024
MIB = 1024 * 1024

ELEM_ROWS = 256
MM_ROWS = 1536
FFN_OUT_ROWS = 512
FFN_TILE = 256


def _params(vmem_bytes, ngrid):
    assert vmem_bytes <= V7X_VMEM_BYTES
    return pltpu.CompilerParams(
        dimension_semantics=("arbitrary",) * ngrid,
        vmem_limit_bytes=int(vmem_bytes),
    )


def _mod_row_of_block(i, rows_per_block):
    ctx_blocks = T_CTX // rows_per_block
    blocks_per_lat = LAT_LEN // rows_per_block
    return jnp.where(i < ctx_blocks, CTX_MOD_ROW, (i - ctx_blocks) // blocks_per_lat)


ADA_TILE = 512


def _ada_kernel(c_ref, w_ref, b_ref, o_ref):
    c = c_ref[...]
    s = (c * jax.nn.sigmoid(c)).astype(BF16)
    w = w_ref[0].astype(BF16)
    o_ref[0] = jnp.dot(s, w, preferred_element_type=F32) + b_ref[0]


def _ada_table(cond, w_ada, b_ada):
    n_out = 6 * D_MODEL
    vmem = 2 * (D_MODEL * ADA_TILE * 4) + D_MODEL * ADA_TILE * 2 + 4 * MIB
    return pl.pallas_call(
        _ada_kernel,
        grid=(DEPTH, n_out // ADA_TILE),
        in_specs=[
            pl.BlockSpec((MOD_ROWS, D_MODEL), lambda l, j: (0, 0)),
            pl.BlockSpec((1, D_MODEL, ADA_TILE), lambda l, j: (l, 0, j)),
            pl.BlockSpec((1, 1, ADA_TILE), lambda l, j: (l, 0, j)),
        ],
        out_specs=pl.BlockSpec((1, MOD_ROWS, ADA_TILE), lambda l, j: (l, 0, j)),
        out_shape=jax.ShapeDtypeStruct((DEPTH, MOD_ROWS, n_out), F32),
        compiler_params=_params(vmem, 2),
        name="ada_table",
    )(cond, w_ada, b_ada.reshape(DEPTH, 1, n_out))


def _rms(x, g):
    return x * lax.rsqrt(jnp.mean(x * x, axis=-1, keepdims=True) + EPS) * g


def _pre(y, g_pre, mod_ref, sub):
    shift = mod_ref[pl.ds(3 * sub, 1), :]
    scale = mod_ref[pl.ds(3 * sub + 1, 1), :]
    return _rms(y, g_pre) * (1.0 + scale) + shift


_CTX_BLOCKS = T_CTX // ELEM_ROWS


def _read_stream(yp_ref, ys_ref):
    return jnp.where(pl.program_id(0) < _CTX_BLOCKS, yp_ref[...], ys_ref[...])


def _pre_kernel(sub, yp_ref, ys_ref, gpre_ref, mod_ref, h_ref):
    h_ref[...] = _pre(_read_stream(yp_ref, ys_ref), gpre_ref[...], mod_ref, sub).astype(h_ref.dtype)


def _post_kernel(sub, o_ref, y_ref, gpost_ref, mod_ref, yp_ref, ys_ref):
    gate = mod_ref[pl.ds(3 * sub + 2, 1), :]
    yn = y_ref[...] + gate * _rms(o_ref[...].astype(F32), gpost_ref[...])

    @pl.when(pl.program_id(0) < _CTX_BLOCKS)
    def _():
        yp_ref[...] = yn

    @pl.when(pl.program_id(0) >= _CTX_BLOCKS)
    def _():
        ys_ref[...] = yn


def _postpre_kernel(sub, nsub, split_in, o_ref, *refs):
    if split_in:
        yp_ref, ys_ref, gpost_ref, mod_ref, gpre_ref, nmod_ref, yo_ref, h_ref = refs
        y = _read_stream(yp_ref, ys_ref)
    else:
        y_ref, gpost_ref, mod_ref, gpre_ref, nmod_ref, yo_ref, h_ref = refs
        y = y_ref[...]
    gate = mod_ref[pl.ds(3 * sub + 2, 1), :]
    yn = y + gate * _rms(o_ref[...].astype(F32), gpost_ref[...])
    yo_ref[...] = yn
    h_ref[...] = _pre(yn, gpre_ref[...], nmod_ref, nsub).astype(h_ref.dtype)


def _row_spec():
    return pl.BlockSpec((ELEM_ROWS, D_MODEL), lambda i: (i, 0))


def _split_specs():
    return [pl.BlockSpec((ELEM_ROWS, D_MODEL), lambda i: (jnp.minimum(i, _CTX_BLOCKS - 1), 0)),
            pl.BlockSpec((ELEM_ROWS, D_MODEL), lambda i: (jnp.maximum(i - _CTX_BLOCKS, 0), 0))]


def _gain_spec(layer):
    return pl.BlockSpec((None, 1, D_MODEL), lambda i: (layer, 0, 0))


def _mod_spec(layer):
    return pl.BlockSpec((None, None, 6, D_MODEL),
                        lambda i: (layer, _mod_row_of_block(i, ELEM_ROWS), 0, 0))


_ELEM_BLOCK_F32 = ELEM_ROWS * D_MODEL * 4


def _pre_call(xp, xs, g_pre, mod, layer, sub):
    vmem = 2 * (2 * _ELEM_BLOCK_F32 + _ELEM_BLOCK_F32 // 2) + 3 * _ELEM_BLOCK_F32 + 2 * MIB
    return pl.pallas_call(
        functools.partial(_pre_kernel, sub),
        grid=(T_ALL // ELEM_ROWS,),
        in_specs=_split_specs() + [_gain_spec(layer), _mod_spec(layer)],
        out_specs=_row_spec(),
        out_shape=jax.ShapeDtypeStruct((T_ALL, D_MODEL), BF16),
        compiler_params=_params(vmem, 1),
        name="pre_norm",
    )(xp, xs, g_pre, mod)


def _post_call(o, y, g_post, mod, layer, sub):
    vmem = 2 * 4 * _ELEM_BLOCK_F32 + 3 * _ELEM_BLOCK_F32 + 2 * MIB
    return pl.pallas_call(
        functools.partial(_post_kernel, sub),
        grid=(T_ALL // ELEM_ROWS,),
        in_specs=[_row_spec(), _row_spec(), _gain_spec(layer), _mod_spec(layer)],
        out_specs=_split_specs(),
        out_shape=[jax.ShapeDtypeStruct((T_CTX, D_MODEL), F32),
                   jax.ShapeDtypeStruct((T_LAT, D_MODEL), F32)],
        compiler_params=_params(vmem, 1),
        name="post_norm",
    )(o, y, g_post, mod)


def _postpre_call(o, y, g_post, mod, layer, sub, g_pre, nlayer, nsub):
    split_in = isinstance(y, tuple)
    y_args = y if split_in else (y,)
    y_specs = _split_specs() if split_in else [_row_spec()]
    vmem = 2 * ((2 + len(y_args)) * _ELEM_BLOCK_F32 + _ELEM_BLOCK_F32 // 2) + 4 * _ELEM_BLOCK_F32 + 2 * MIB
    return pl.pallas_call(
        functools.partial(_postpre_kernel, sub, nsub, split_in),
        grid=(T_ALL // ELEM_ROWS,),
        in_specs=[_row_spec()] + y_specs + [_gain_spec(layer), _mod_spec(layer),
                                            _gain_spec(nlayer), _mod_spec(nlayer)],
        out_specs=[_row_spec(), _row_spec()],
        out_shape=[jax.ShapeDtypeStruct((T_ALL, D_MODEL), F32),
                   jax.ShapeDtypeStruct((T_ALL, D_MODEL), BF16)],
        compiler_params=_params(vmem, 1),
        name="post_pre_norm",
    )(o, *y_args, g_post, mod, g_pre, mod)


def _mm_kernel(x_ref, w_ref, o_ref):
    o_ref[...] = jnp.dot(x_ref[...], w_ref[...], preferred_element_type=F32).astype(o_ref.dtype)


def _matmul_rows_outer(x, w_stack, layer, *, rows, cols, out_dtype, name):
    m, k = x.shape
    n = w_stack.shape[2]
    out_bytes = jnp.dtype(out_dtype).itemsize
    vmem = 2 * (rows * k * 2 + k * cols * 2 + rows * cols * out_bytes) + 2 * rows * cols * 4 + 2 * MIB
    return pl.pallas_call(
        _mm_kernel,
        grid=(m // rows, n // cols),
        in_specs=[pl.BlockSpec((rows, k), lambda i, j: (i, 0)),
                  pl.BlockSpec((None, k, cols), lambda i, j: (layer, 0, j))],
        out_specs=pl.BlockSpec((rows, cols), lambda i, j: (i, j)),
        out_shape=jax.ShapeDtypeStruct((m, n), out_dtype),
        compiler_params=_params(vmem, 2),
        name=name,
    )(x, w_stack)


def _mm_wcast_kernel(x_ref, w_ref, o_ref, wb_ref):
    @pl.when(pl.program_id(1) == 0)
    def _():
        wb_ref[...] = w_ref[...].astype(BF16)

    o_ref[...] = jnp.dot(x_ref[...], wb_ref[...], preferred_element_type=F32).astype(o_ref.dtype)


def _matmul_cols_outer(x, w_stack, layer, *, cols, out_dtype, name):
    m, k = x.shape
    n = w_stack.shape[2]
    rows = MM_ROWS
    out_bytes = jnp.dtype(out_dtype).itemsize
    vmem = (2 * (rows * k * 2 + k * cols * 4 + rows * cols * out_bytes) + k * cols * 2
            + 2 * rows * cols * 4 + 2 * MIB)
    return pl.pallas_call(
        _mm_wcast_kernel,
        grid=(n // cols, m // rows),
        in_specs=[pl.BlockSpec((rows, k), lambda j, i: (i, 0)),
                  pl.BlockSpec((None, k, cols), lambda j, i: (layer, 0, j))],
        out_specs=pl.BlockSpec((rows, cols), lambda j, i: (i, j)),
        out_shape=jax.ShapeDtypeStruct((m, n), out_dtype),
        scratch_shapes=[pltpu.VMEM((k, cols), BF16)],
        compiler_params=_params(vmem, 2),
        name=name,
    )(x, w_stack)


def _swiglu_in_kernel(x_ref, wg_ref, wu_ref, o_ref, wb_ref):
    @pl.when(pl.program_id(1) == 0)
    def _():
        wb_ref[:, :FFN_TILE] = wg_ref[...].astype(BF16)
        wb_ref[:, FFN_TILE:] = wu_ref[...].astype(BF16)

    gu = jnp.dot(x_ref[...], wb_ref[...], preferred_element_type=F32)
    g = gu[:, :FFN_TILE]
    u = gu[:, FFN_TILE:]
    o_ref[...] = (g * jax.nn.sigmoid(g) * u).astype(o_ref.dtype)


def _swiglu_in(h, w_in_stack, layer):
    n_tiles = D_FF // FFN_TILE
    vmem = (2 * (MM_ROWS * D_MODEL * 2 + 2 * D_MODEL * FFN_TILE * 4 + MM_ROWS * FFN_TILE * 2)
            + D_MODEL * 2 * FFN_TILE * 2 + 4 * MM_ROWS * FFN_TILE * 4 + 2 * MIB)
    return pl.pallas_call(
        _swiglu_in_kernel,
        grid=(n_tiles, T_ALL // MM_ROWS),
        in_specs=[pl.BlockSpec((MM_ROWS, D_MODEL), lambda j, i: (i, 0)),
                  pl.BlockSpec((None, D_MODEL, FFN_TILE), lambda j, i: (layer, 0, j)),
                  pl.BlockSpec((None, D_MODEL, FFN_TILE), lambda j, i: (layer, 0, n_tiles + j))],
        out_specs=pl.BlockSpec((MM_ROWS, FFN_TILE), lambda j, i: (i, j)),
        out_shape=jax.ShapeDtypeStruct((T_ALL, D_FF), BF16),
        scratch_shapes=[pltpu.VMEM((D_MODEL, 2 * FFN_TILE), BF16)],
        compiler_params=_params(vmem, 2),
        name="ffn_in_swiglu",
    )(h, w_in_stack, w_in_stack)


def _attn_ctx_kernel(sink_ref, q_ref, k_ref, v_ref, prev_ref, o_ref):
    del prev_ref
    kh = pl.program_id(1)
    kt = k_ref[...].T.astype(BF16)
    v = v_ref[...].astype(BF16)
    for g in range(KV_GROUP):
        cols = slice(g * HEAD_DIM, (g + 1) * HEAD_DIM)
        q = (q_ref[:, cols] * Q_SCALE).astype(BF16)
        s = jnp.dot(q, kt, preferred_element_type=F32)
        sk = sink_ref[kh * KV_GROUP + g] * LOG2E
        m = jnp.maximum(jnp.max(_fold_lanes(s, jnp.maximum), axis=-1, keepdims=True), sk)
        p = jnp.exp2(s - m)
        denom = jnp.sum(_fold_lanes(p, jnp.add), axis=-1, keepdims=True) + jnp.exp2(sk - m)
        o = jnp.dot(p.astype(BF16), v, preferred_element_type=F32)
        o_ref[:, cols] = (o / denom).astype(o_ref.dtype)


def _fold_lanes(x, op):
    tiles = [x[:, t * HEAD_DIM:(t + 1) * HEAD_DIM] for t in range(x.shape[1] // HEAD_DIM)]
    while len(tiles) > 1:
        tiles = [op(tiles[t], tiles[t + 1]) for t in range(0, len(tiles), 2)]
    return tiles[0]


def _rope(x, cos, sin_lo, sin_hi):
    return x * cos + pltpu.roll(x, 96, 1) * sin_lo + pltpu.roll(x, 32, 1) * sin_hi


def _attn_lat_kernel(sink_ref, q_ref, k_ref, v_ref, ck_ref, cv_ref, cos_ref, slo_ref, shi_ref,
                     cosq_ref, sloq_ref, shiq_ref, o_ref, kt_ref, vb_ref, ckt_ref, cvb_ref):
    kh = pl.program_id(1)
    n_blocks = LAT_LEN // ATTN_BLOCK
    for n in range(n_blocks):
        blk = slice(n * ATTN_BLOCK, (n + 1) * ATTN_BLOCK)
        kt_ref[n] = _rope(k_ref[blk, :], cos_ref[blk, :], slo_ref[blk, :], shi_ref[blk, :]).T.astype(BF16)
    vb_ref[...] = v_ref[...].astype(BF16)
    ckt_ref[...] = ck_ref[...].T.astype(BF16)
    cvb_ref[...] = cv_ref[...].astype(BF16)

    rows = KV_GROUP * ATTN_BLOCK
    kq = (lax.broadcasted_iota(jnp.int32, (rows, ATTN_BLOCK), 1)
          - lax.broadcasted_iota(jnp.int32, (rows, ATTN_BLOCK), 0) % ATTN_BLOCK)
    sk = jnp.concatenate(
        [jnp.full((ATTN_BLOCK, 1), sink_ref[kh * KV_GROUP + g] * LOG2E, F32) for g in range(KV_GROUP)],
        axis=0)

    def scores(n):
        r0 = pl.multiple_of(n * ATTN_BLOCK, ATTN_BLOCK)
        cos = cosq_ref[pl.ds(r0, ATTN_BLOCK), :]
        slo = sloq_ref[pl.ds(r0, ATTN_BLOCK), :]
        shi = shiq_ref[pl.ds(r0, ATTN_BLOCK), :]
        q = jnp.concatenate(
            [_rope(q_ref[pl.ds(r0, ATTN_BLOCK), g * HEAD_DIM:(g + 1) * HEAD_DIM], cos, slo, shi).astype(BF16)
             for g in range(KV_GROUP)], axis=0)
        n_prev = jnp.maximum(n - 1, 0)
        n_next = jnp.minimum(n + 1, n_blocks - 1)
        thr_p = jnp.where(n >= 1, 0, 2 * ATTN_BLOCK)
        thr_n = jnp.where(n <= n_blocks - 2, 0, -2 * ATTN_BLOCK)
        s_p = jnp.where(kq >= thr_p, jnp.dot(q, kt_ref[n_prev], preferred_element_type=F32), NEG)
        s_0 = jnp.dot(q, kt_ref[n], preferred_element_type=F32)
        s_n = jnp.where(kq <= thr_n, jnp.dot(q, kt_ref[n_next], preferred_element_type=F32), NEG)
        s_c = jnp.dot(q, ckt_ref[...], preferred_element_type=F32)
        return s_p, s_0, s_n, s_c

    def softmax(s_p, s_0, s_n, s_c):
        m = jnp.maximum(jnp.maximum(s_p, s_0), jnp.maximum(s_n, _fold_lanes(s_c, jnp.maximum)))
        m = jnp.maximum(jnp.max(m, axis=-1, keepdims=True), sk)
        p_p = jnp.exp2(s_p - m)
        p_0 = jnp.exp2(s_0 - m)
        p_n = jnp.exp2(s_n - m)
        p_c = jnp.exp2(s_c - m)
        denom = (jnp.sum((p_p + p_0) + (p_n + _fold_lanes(p_c, jnp.add)), axis=-1, keepdims=True)
                 + jnp.exp2(sk - m))
        return p_p.astype(BF16), p_0.astype(BF16), p_n.astype(BF16), p_c.astype(BF16), denom

    def weighted_values(n, p_p, p_0, p_n, p_c, denom):
        r0 = pl.multiple_of(n * ATTN_BLOCK, ATTN_BLOCK)
        rp = pl.multiple_of(jnp.maximum(n - 1, 0) * ATTN_BLOCK, ATTN_BLOCK)
        rn = pl.multiple_of(jnp.minimum(n + 1, n_blocks - 1) * ATTN_BLOCK, ATTN_BLOCK)
        o = (jnp.dot(p_p, vb_ref[pl.ds(rp, ATTN_BLOCK), :], preferred_element_type=F32)
             + jnp.dot(p_0, vb_ref[pl.ds(r0, ATTN_BLOCK), :], preferred_element_type=F32)
             + jnp.dot(p_n, vb_ref[pl.ds(rn, ATTN_BLOCK), :], preferred_element_type=F32)
             + jnp.dot(p_c, cvb_ref[...], preferred_element_type=F32))
        o = o / denom
        for g in range(KV_GROUP):
            o_ref[pl.ds(r0, ATTN_BLOCK), g * HEAD_DIM:(g + 1) * HEAD_DIM] = (
                o[g * ATTN_BLOCK:(g + 1) * ATTN_BLOCK, :].astype(o_ref.dtype))

    def group(i, carry):
        blocks = [i * ATTN_GROUP + j for j in range(ATTN_GROUP)]
        s = [scores(n) for n in blocks]
        p = [softmax(*sn) for sn in s]
        for n, pn in zip(blocks, p):
            weighted_values(n, *pn)
        return carry

    lax.fori_loop(0, n_blocks // ATTN_GROUP, group, 0)


def _rope_tables():
    pos = jnp.arange(LAT_LEN, dtype=jnp.int32)
    row = (pos // GRID_W).astype(F32)
    col = (pos % GRID_W).astype(F32)
    half = HEAD_DIM // 2
    inv_freq = ROPE_BASE ** (-jnp.arange(0, half, 2, dtype=F32) / half)
    ang_r = row[:, None] * inv_freq[None, :]
    ang_c = col[:, None] * inv_freq[None, :]
    zero = jnp.zeros_like(ang_r)
    cos = jnp.concatenate([jnp.cos(ang_r), jnp.cos(ang_r), jnp.cos(ang_c), jnp.cos(ang_c)], axis=1)
    sin_lo = jnp.concatenate([-jnp.sin(ang_r), zero, -jnp.sin(ang_c), zero], axis=1)
    sin_hi = jnp.concatenate([zero, jnp.sin(ang_r), zero, jnp.sin(ang_c)], axis=1)
    return cos, sin_lo, sin_hi


def _attention(qkv, cache_k, cache_v, sink, layer_a):
    q_cols = KV_GROUP * HEAD_DIM
    k_blk0 = N_HEADS
    v_blk0 = N_HEADS + N_KV_HEADS
    ctx_rows0 = T_CTX // LAT_LEN
    cos, sin_lo, sin_hi = _rope_tables()
    q_tabs = [t * Q_SCALE for t in (cos, sin_lo, sin_hi)]
    ck = cache_k.reshape(N_LAT_REQ, -1, PAST_LEN, N_KV_HEADS * HEAD_DIM)
    cv = cache_v.reshape(N_LAT_REQ, -1, PAST_LEN, N_KV_HEADS * HEAD_DIM)
    tab_spec = pl.BlockSpec((LAT_LEN, HEAD_DIM), lambda b, h: (0, 0))
    smem_spec = pl.BlockSpec(memory_space=pltpu.SMEM)
    lat_block = LAT_LEN * HEAD_DIM * 4
    vmem_lat = (2 * (q_cols // HEAD_DIM + 2 + 1 + 6) * lat_block + 2 * LAT_LEN * q_cols * 2
                + 3 * lat_block + 16 * MIB)
    o_lat = pl.pallas_call(
        _attn_lat_kernel,
        grid=(N_LAT_REQ, N_KV_HEADS),
        in_specs=[
            smem_spec,
            pl.BlockSpec((LAT_LEN, q_cols), lambda b, h: (ctx_rows0 + b, h)),
            pl.BlockSpec((LAT_LEN, HEAD_DIM), lambda b, h: (ctx_rows0 + b, k_blk0 + h)),
            pl.BlockSpec((LAT_LEN, HEAD_DIM), lambda b, h: (ctx_rows0 + b, v_blk0 + h)),
            pl.BlockSpec((None, None, PAST_LEN, HEAD_DIM), lambda b, h: (b, layer_a, 0, h)),
            pl.BlockSpec((None, None, PAST_LEN, HEAD_DIM), lambda b, h: (b, layer_a, 0, h)),
            tab_spec, tab_spec, tab_spec, tab_spec, tab_spec, tab_spec,
        ],
        out_specs=pl.BlockSpec((LAT_LEN, q_cols), lambda b, h: (ctx_rows0 + b, h)),
        out_shape=jax.ShapeDtypeStruct((T_ALL, N_HEADS * HEAD_DIM), BF16),
        scratch_shapes=[pltpu.VMEM((LAT_LEN // ATTN_BLOCK, HEAD_DIM, ATTN_BLOCK), BF16),
                        pltpu.VMEM((LAT_LEN, HEAD_DIM), BF16),
                        pltpu.VMEM((HEAD_DIM, PAST_LEN), BF16), pltpu.VMEM((PAST_LEN, HEAD_DIM), BF16)],
        compiler_params=_params(vmem_lat, 2),
        name="attn_latent",
    )(sink, qkv, qkv, qkv, ck, cv, cos, sin_lo, sin_hi, *q_tabs)

    vmem_ctx = 2 * (CTX_LEN * q_cols * 4 + 2 * CTX_LEN * HEAD_DIM * 4 + CTX_LEN * q_cols * 2) + 8 * MIB
    return pl.pallas_call(
        _attn_ctx_kernel,
        grid=(N_CTX_REQ, N_KV_HEADS),
        in_specs=[
            smem_spec,
            pl.BlockSpec((CTX_LEN, q_cols), lambda b, h: (b, h)),
            pl.BlockSpec((CTX_LEN, HEAD_DIM), lambda b, h: (b, k_blk0 + h)),
            pl.BlockSpec((CTX_LEN, HEAD_DIM), lambda b, h: (b, v_blk0 + h)),
            pl.BlockSpec(memory_space=pl.ANY),
        ],
        out_specs=pl.BlockSpec((CTX_LEN, q_cols), lambda b, h: (b, h)),
        out_shape=jax.ShapeDtypeStruct((T_ALL, N_HEADS * HEAD_DIM), BF16),
        input_output_aliases={4: 0},
        compiler_params=_params(vmem_ctx, 2),
        name="attn_context",
    )(sink, qkv, qkv, qkv, o_lat)


def _dft_tables(n):
    k = jnp.arange(n, dtype=jnp.int32)
    ang = ((k[:, None] * k[None, :]) % n).astype(F32) * (2.0 * math.pi / n)
    return jnp.cos(ang), jnp.sin(ang)


def _fourier_group(x, cs_ref, pos_ref, norm):
    z = jnp.dot(x, cs_ref[...], preferred_element_type=F32)
    zz = jnp.concatenate([z[:, :GROUP_DIM], z[:, GROUP_DIM:]], axis=0).astype(BF16)
    return jnp.dot(pos_ref[...], zz, preferred_element_type=F32) * norm


def _fourier_lat_kernel(x_ref, cs_ref, pos_ref, o_ref):
    norm = (LAT_LEN * GROUP_DIM) ** -0.5
    o_ref[...] = _fourier_group(x_ref[...], cs_ref, pos_ref, norm).astype(o_ref.dtype)


def _fourier_ctx_kernel(x_ref, cs_ref, pos_ref, prev_ref, o_ref):
    del prev_ref
    norm = (CTX_LEN * GROUP_DIM) ** -0.5
    for g in range(N_GROUPS):
        cols = slice(g * GROUP_DIM, (g + 1) * GROUP_DIM)
        o_ref[:, cols] = _fourier_group(x_ref[:, cols], cs_ref, pos_ref, norm).astype(o_ref.dtype)


def _fourier(h):
    cc, sc = _dft_tables(GROUP_DIM)
    cs = jnp.concatenate([cc, sc], axis=1).astype(BF16)
    cl, sl = _dft_tables(LAT_LEN)
    pos_lat = jnp.concatenate([cl, -sl], axis=1).astype(BF16)
    cp, sp = _dft_tables(CTX_LEN)
    pos_ctx = jnp.concatenate([cp, -sp], axis=1).astype(BF16)
    ctx_rows0 = T_CTX // LAT_LEN

    blk = LAT_LEN * GROUP_DIM
    vmem_lat = 2 * (2 * blk * 2 + GROUP_DIM * 2 * GROUP_DIM * 2 + LAT_LEN * 2 * LAT_LEN * 2) + 5 * blk * 4 + 4 * MIB
    f_lat = pl.pallas_call(
        _fourier_lat_kernel,
        grid=(N_LAT_REQ, N_GROUPS),
        in_specs=[pl.BlockSpec((LAT_LEN, GROUP_DIM), lambda b, g: (ctx_rows0 + b, g)),
                  pl.BlockSpec((GROUP_DIM, 2 * GROUP_DIM), lambda b, g: (0, 0)),
                  pl.BlockSpec((LAT_LEN, 2 * LAT_LEN), lambda b, g: (0, 0))],
        out_specs=pl.BlockSpec((LAT_LEN, GROUP_DIM), lambda b, g: (ctx_rows0 + b, g)),
        out_shape=jax.ShapeDtypeStruct((T_ALL, D_MODEL), BF16),
        compiler_params=_params(vmem_lat, 2),
        name="fourier_latent",
    )(h, cs, pos_lat)

    vmem_ctx = 2 * (2 * CTX_LEN * D_MODEL * 2 + GROUP_DIM * 2 * GROUP_DIM * 2 + CTX_LEN * 2 * CTX_LEN * 2) + 16 * MIB
    return pl.pallas_call(
        _fourier_ctx_kernel,
        grid=(N_CTX_REQ,),
        in_specs=[pl.BlockSpec((CTX_LEN, D_MODEL), lambda b: (b, 0)),
                  pl.BlockSpec((GROUP_DIM, 2 * GROUP_DIM), lambda b: (0, 0)),
                  pl.BlockSpec((CTX_LEN, 2 * CTX_LEN), lambda b: (0, 0)),
                  pl.BlockSpec(memory_space=pl.ANY)],
        out_specs=pl.BlockSpec((CTX_LEN, D_MODEL), lambda b: (b, 0)),
        out_shape=jax.ShapeDtypeStruct((T_ALL, D_MODEL), BF16),
        input_output_aliases={3: 0},
        compiler_params=_params(vmem_ctx, 1),
        name="fourier_context",
    )(h, cs, pos_ctx, f_lat)


def kernel(x_prompt, x_sample, cache_k, cache_v, c, c_ctx, w_ada, b_ada, g_mix_pre, g_mix_post,
           g_ffn_pre, g_ffn_post, w_qkv, w_attn_out, attn_sink, w_fourier, w_ffn_in, w_ffn_out):
    xp = x_prompt.reshape(T_CTX, D_MODEL)
    xs = x_sample.reshape(T_LAT, D_MODEL)
    cond = jnp.concatenate(
        [c, c_ctx[None, :], jnp.zeros((MOD_ROWS - N_LAT_REQ - 1, D_MODEL), F32)], axis=0)
    mod = _ada_table(cond, w_ada, b_ada).reshape(DEPTH, MOD_ROWS, 6, D_MODEL)
    g_mix_pre, g_mix_post, g_ffn_pre, g_ffn_post = (
        g.reshape(DEPTH, 1, D_MODEL) for g in (g_mix_pre, g_mix_post, g_ffn_pre, g_ffn_post))
    w_ffn_out_b = w_ffn_out.astype(BF16)

    new_k, new_v = [], []
    y = (xp, xs)
    h = _pre_call(xp, xs, g_mix_pre, mod, 0, 0)
    for i in range(DEPTH):
        if i % 2 == 0:
            a = i // 2
            qkv = _matmul_cols_outer(h, w_qkv, a, cols=512, out_dtype=F32, name="qkv_proj")
            k0 = N_HEADS * HEAD_DIM
            k1 = k0 + N_KV_HEADS * HEAD_DIM
            new_k.append(qkv[:T_CTX, k0:k1].reshape(N_CTX_REQ, CTX_LEN, N_KV_HEADS, HEAD_DIM))
            new_v.append(qkv[:T_CTX, k1:].reshape(N_CTX_REQ, CTX_LEN, N_KV_HEADS, HEAD_DIM))
            att = _attention(qkv, cache_k, cache_v, attn_sink[a], a)
            o = _matmul_cols_outer(att, w_attn_out, a, cols=512, out_dtype=BF16, name="attn_out_proj")
        else:
            o = _matmul_cols_outer(_fourier(h), w_fourier, i // 2, cols=512, out_dtype=BF16,
                                   name="fourier_proj")
        y, h = _postpre_call(o, y, g_mix_post, mod, i, 0, g_ffn_pre, i, 1)
        act = _swiglu_in(h, w_ffn_in, i)
        o = _matmul_rows_outer(act, w_ffn_out_b, i, rows=FFN_OUT_ROWS, cols=512, out_dtype=BF16,
                               name="ffn_out_proj")
        if i + 1 < DEPTH:
            y, h = _postpre_call(o, y, g_ffn_post, mod, i, 1, g_mix_pre, i + 1, 0)
        else:
            y_p, y_s = _post_call(o, y, g_ffn_post, mod, i, 1)

    return (y_p.reshape(N_CTX_REQ, CTX_LEN, D_MODEL), y_s.reshape(N_LAT_REQ, LAT_LEN, D_MODEL),
            jnp.stack(new_k, axis=1), jnp.stack(new_v, axis=1))
```

```python
import functools
import math

import jax
import jax.numpy as jnp
from jax import lax
from jax.experimental import pallas as pl
from jax.experimental.pallas import tpu as pltpu

F32 = jnp.float32
BF16 = jnp.bfloat16

D_MODEL = 4096
N_CTX_REQ = 16
CTX_LEN = 256
N_LAT_REQ = 8
LAT_LEN = 1024
PAST_LEN = 512
DEPTH = 4
GRID_W = 64
HEAD_DIM = 128
N_HEADS = 32
N_KV_HEADS = 8
KV_GROUP = 4
ATTN_BLOCK = 128
ATTN_GROUP = 4
ROPE_BASE = 10000.0
N_GROUPS = 8
GROUP_DIM = D_MODEL // N_GROUPS
D_FF = 11008
EPS = 1e-6
NEG = -1e30

T_CTX = N_CTX_REQ * CTX_LEN
T_LAT = N_LAT_REQ * LAT_LEN
T_ALL = T_CTX + T_LAT
MOD_ROWS = 16
CTX_MOD_ROW = N_LAT_REQ
QKV_W = (N_HEADS + 2 * N_KV_HEADS) * HEAD_DIM
ATTN_SCALE = HEAD_DIM ** -0.5
LOG2E = math.log2(math.e)
Q_SCALE = ATTN_SCALE * LOG2E

V7X_VMEM_BYTES = 64 * 1024 * 1024
MIB = 1024 * 1024

ELEM_ROWS = 256
MM_ROWS = 1024
FFN_IN_ROWS = 2048
FFN_OUT_ROWS = 512
FFN_TILE = 256


def _params(vmem_bytes, ngrid):
    assert vmem_bytes <= V7X_VMEM_BYTES
    return pltpu.CompilerParams(
        dimension_semantics=("arbitrary",) * ngrid,
        vmem_limit_bytes=int(vmem_bytes),
    )


def _mod_row_of_block(i, rows_per_block):
    ctx_blocks = T_CTX // rows_per_block
    blocks_per_lat = LAT_LEN // rows_per_block
    return jnp.where(i < ctx_blocks, CTX_MOD_ROW, (i - ctx_blocks) // blocks_per_lat)


ADA_TILE = 512


def _ada_kernel(c_ref, w_ref, b_ref, o_ref):
    c = c_ref[...]
    s = (c * jax.nn.sigmoid(c)).astype(BF16)
    w = w_ref[0].astype(BF16)
    o_ref[0] = jnp.dot(s, w, preferred_element_type=F32) + b_ref[0]


def _ada_table(cond, w_ada, b_ada):
    n_out = 6 * D_MODEL
    vmem = 2 * (D_MODEL * ADA_TILE * 4) + D_MODEL * ADA_TILE * 2 + 4 * MIB
    return pl.pallas_call(
        _ada_kernel,
        grid=(DEPTH, n_out // ADA_TILE),
        in_specs=[
            pl.BlockSpec((MOD_ROWS, D_MODEL), lambda l, j: (0, 0)),
            pl.BlockSpec((1, D_MODEL, ADA_TILE), lambda l, j: (l, 0, j)),
            pl.BlockSpec((1, 1, ADA_TILE), lambda l, j: (l, 0, j)),
        ],
        out_specs=pl.BlockSpec((1, MOD_ROWS, ADA_TILE), lambda l, j: (l, 0, j)),
        out_shape=jax.ShapeDtypeStruct((DEPTH, MOD_ROWS, n_out), F32),
        compiler_params=_params(vmem, 2),
        name="ada_table",
    )(cond, w_ada, b_ada.reshape(DEPTH, 1, n_out))


def _rms(x, g):
    return x * lax.rsqrt(jnp.mean(x * x, axis=-1, keepdims=True) + EPS) * g


def _pre(y, g_pre, mod_ref, sub):
    shift = mod_ref[pl.ds(3 * sub, 1), :]
    scale = mod_ref[pl.ds(3 * sub + 1, 1), :]
    return _rms(y, g_pre) * (1.0 + scale) + shift


_CTX_BLOCKS = T_CTX // ELEM_ROWS


def _read_stream(yp_ref, ys_ref):
    return jnp.where(pl.program_id(0) < _CTX_BLOCKS, yp_ref[...], ys_ref[...])


def _pre_kernel(sub, yp_ref, ys_ref, gpre_ref, mod_ref, h_ref):
    h_ref[...] = _pre(_read_stream(yp_ref, ys_ref), gpre_ref[...], mod_ref, sub).astype(h_ref.dtype)


def _post_kernel(sub, o_ref, y_ref, gpost_ref, mod_ref, yp_ref, ys_ref):
    gate = mod_ref[pl.ds(3 * sub + 2, 1), :]
    yn = y_ref[...] + gate * _rms(o_ref[...].astype(F32), gpost_ref[...])

    @pl.when(pl.program_id(0) < _CTX_BLOCKS)
    def _():
        yp_ref[...] = yn

    @pl.when(pl.program_id(0) >= _CTX_BLOCKS)
    def _():
        ys_ref[...] = yn


def _postpre_kernel(sub, nsub, split_in, o_ref, *refs):
    if split_in:
        yp_ref, ys_ref, gpost_ref, mod_ref, gpre_ref, nmod_ref, yo_ref, h_ref = refs
        y = _read_stream(yp_ref, ys_ref)
    else:
        y_ref, gpost_ref, mod_ref, gpre_ref, nmod_ref, yo_ref, h_ref = refs
        y = y_ref[...]
    gate = mod_ref[pl.ds(3 * sub + 2, 1), :]
    yn = y + gate * _rms(o_ref[...].astype(F32), gpost_ref[...])
    yo_ref[...] = yn
    h_ref[...] = _pre(yn, gpre_ref[...], nmod_ref, nsub).astype(h_ref.dtype)


def _row_spec():
    return pl.BlockSpec((ELEM_ROWS, D_MODEL), lambda i: (i, 0))


def _split_specs():
    return [pl.BlockSpec((ELEM_ROWS, D_MODEL), lambda i: (jnp.minimum(i, _CTX_BLOCKS - 1), 0)),
            pl.BlockSpec((ELEM_ROWS, D_MODEL), lambda i: (jnp.maximum(i - _CTX_BLOCKS, 0), 0))]


def _gain_spec(layer):
    return pl.BlockSpec((None, 1, D_MODEL), lambda i: (layer, 0, 0))


def _mod_spec(layer):
    return pl.BlockSpec((None, None, 6, D_MODEL),
                        lambda i: (layer, _mod_row_of_block(i, ELEM_ROWS), 0, 0))


_ELEM_BLOCK_F32 = ELEM_ROWS * D_MODEL * 4


def _pre_call(xp, xs, g_pre, mod, layer, sub):
    vmem = 2 * (2 * _ELEM_BLOCK_F32 + _ELEM_BLOCK_F32 // 2) + 3 * _ELEM_BLOCK_F32 + 2 * MIB
    return pl.pallas_call(
        functools.partial(_pre_kernel, sub),
        grid=(T_ALL // ELEM_ROWS,),
        in_specs=_split_specs() + [_gain_spec(layer), _mod_spec(layer)],
        out_specs=_row_spec(),
        out_shape=jax.ShapeDtypeStruct((T_ALL, D_MODEL), BF16),
        compiler_params=_params(vmem, 1),
        name="pre_norm",
    )(xp, xs, g_pre, mod)


def _post_call(o, y, g_post, mod, layer, sub):
    vmem = 2 * 4 * _ELEM_BLOCK_F32 + 3 * _ELEM_BLOCK_F32 + 2 * MIB
    return pl.pallas_call(
        functools.partial(_post_kernel, sub),
        grid=(T_ALL // ELEM_ROWS,),
        in_specs=[_row_spec(), _row_spec(), _gain_spec(layer), _mod_spec(layer)],
        out_specs=_split_specs(),
        out_shape=[jax.ShapeDtypeStruct((T_CTX, D_MODEL), F32),
                   jax.ShapeDtypeStruct((T_LAT, D_MODEL), F32)],
        compiler_params=_params(vmem, 1),
        name="post_norm",
    )(o, y, g_post, mod)


def _postpre_call(o, y, g_post, mod, layer, sub, g_pre, nlayer, nsub):
    split_in = isinstance(y, tuple)
    y_args = y if split_in else (y,)
    y_specs = _split_specs() if split_in else [_row_spec()]
    vmem = 2 * ((2 + len(y_args)) * _ELEM_BLOCK_F32 + _ELEM_BLOCK_F32 // 2) + 4 * _ELEM_BLOCK_F32 + 2 * MIB
    return pl.pallas_call(
        functools.partial(_postpre_kernel, sub, nsub, split_in),
        grid=(T_ALL // ELEM_ROWS,),
        in_specs=[_row_spec()] + y_specs + [_gain_spec(layer), _mod_spec(layer),
                                            _gain_spec(nlayer), _mod_spec(nlayer)],
        out_specs=[_row_spec(), _row_spec()],
        out_shape=[jax.ShapeDtypeStruct((T_ALL, D_MODEL), F32),
                   jax.ShapeDtypeStruct((T_ALL, D_MODEL), BF16)],
        compiler_params=_params(vmem, 1),
        name="post_pre_norm",
    )(o, *y_args, g_post, mod, g_pre, mod)


STREAM_SLOTS = 2


def _aligned(v, m):
    return v if isinstance(v, int) else pl.multiple_of(v, m)


def _wstream_kernel(layer, seg_starts, seg_width, chunk_rows, n_chunks, epilogue,
                    x_ref, w_hbm, o_ref, wb_ref, stage_ref, sem):
    j = pl.program_id(0)
    i = pl.program_id(1)
    nj = pl.num_programs(0)
    ni = pl.num_programs(1)
    cur = lax.rem(j, 2)
    nxt = 1 - cur

    def chunk_copies(tile, c, slot):
        r0 = _aligned(c * chunk_rows, chunk_rows)
        return [pltpu.make_async_copy(
            w_hbm.at[layer, pl.ds(r0, chunk_rows), pl.ds(_aligned(start(tile), seg_width), seg_width)],
            stage_ref.at[slot, :, pl.ds(s * seg_width, seg_width)],
            sem.at[slot]) for s, start in enumerate(seg_starts)]

    def round_chunk(dst, c, slot):
        r0 = _aligned(c * chunk_rows, chunk_rows)
        wb_ref[dst, pl.ds(r0, chunk_rows), :] = stage_ref[slot].astype(BF16)

    @pl.when((j == 0) & (i == 0))
    def _():
        for cp in chunk_copies(0, 0, 0):
            cp.start()
        for c in range(n_chunks):
            if c + 1 < n_chunks:
                for cp in chunk_copies(0, c + 1, (c + 1) % STREAM_SLOTS):
                    cp.start()
            for cp in chunk_copies(0, c, c % STREAM_SLOTS):
                cp.wait()
            round_chunk(0, c, c % STREAM_SLOTS)

    prefetching = j < nj - 1

    @pl.when(prefetching & (i >= 1))
    def _():
        for slot in range(STREAM_SLOTS):
            for cp in chunk_copies(0, 0, slot):
                cp.wait()

    for slot in range(STREAM_SLOTS):
        c = jnp.minimum(jnp.maximum(i - 1, 0) * STREAM_SLOTS + slot, n_chunks - 1)
        round_chunk(nxt, c, slot)

    acc = jnp.dot(x_ref[...], wb_ref[cur], preferred_element_type=F32)
    o_ref[...] = epilogue(acc).astype(o_ref.dtype)

    @pl.when(prefetching & (i <= ni - 2))
    def _():
        for slot in range(STREAM_SLOTS):
            c = jnp.minimum(i * STREAM_SLOTS + slot, n_chunks - 1)
            for cp in chunk_copies(j + 1, c, slot):
                cp.start()


def _matmul_wstream(x, w_stack, layer, *, rows, seg_starts, seg_width, out_cols, chunk_rows, out_dtype,
                    epilogue, name):
    m, k = x.shape
    cols = len(seg_starts) * seg_width
    n_tiles = (w_stack.shape[2] // cols)
    n_row_blocks = m // rows
    n_chunks = k // chunk_rows
    assert m % rows == 0 and k % chunk_rows == 0 and w_stack.shape[2] % cols == 0
    assert (n_row_blocks - 1) * STREAM_SLOTS >= n_chunks >= STREAM_SLOTS
    out_bytes = jnp.dtype(out_dtype).itemsize
    vmem = (2 * rows * k * 2 + 2 * k * cols * 2 + STREAM_SLOTS * chunk_rows * cols * 4
            + 2 * rows * out_cols * out_bytes + 2 * rows * cols * 4 + 2 * MIB)
    return pl.pallas_call(
        functools.partial(_wstream_kernel, layer, seg_starts, seg_width, chunk_rows, n_chunks, epilogue),
        grid=(n_tiles, n_row_blocks),
        in_specs=[pl.BlockSpec((rows, k), lambda j, i: (i, 0)),
                  pl.BlockSpec(memory_space=pl.ANY)],
        out_specs=pl.BlockSpec((rows, out_cols), lambda j, i: (i, j)),
        out_shape=jax.ShapeDtypeStruct((m, n_tiles * out_cols), out_dtype),
        scratch_shapes=[pltpu.VMEM((2, k, cols), BF16),
                        pltpu.VMEM((STREAM_SLOTS, chunk_rows, cols), F32),
                        pltpu.SemaphoreType.DMA((STREAM_SLOTS,))],
        compiler_params=_params(vmem, 2),
        name=name,
    )(x, w_stack)


def _identity(acc):
    return acc


def _project(x, w_stack, layer, *, rows, cols, chunk_rows, out_dtype, name):
    return _matmul_wstream(x, w_stack, layer, rows=rows, seg_starts=[lambda t: t * cols], seg_width=cols,
                           out_cols=cols, chunk_rows=chunk_rows, out_dtype=out_dtype, epilogue=_identity,
                           name=name)


def _swiglu(acc):
    g = acc[:, :FFN_TILE]
    u = acc[:, FFN_TILE:]
    return g * jax.nn.sigmoid(g) * u


def _swiglu_in(h, w_in_stack, layer):
    return _matmul_wstream(
        h, w_in_stack, layer, rows=FFN_IN_ROWS,
        seg_starts=[lambda t: t * FFN_TILE, lambda t: D_FF + t * FFN_TILE], seg_width=FFN_TILE,
        out_cols=FFN_TILE, chunk_rows=512, out_dtype=BF16, epilogue=_swiglu, name="ffn_in_swiglu")


def _attn_ctx_kernel(sink_ref, q_ref, k_ref, v_ref, prev_ref, o_ref):
    del prev_ref
    kh = pl.program_id(1)
    kt = k_ref[...].T.astype(BF16)
    v = v_ref[...].astype(BF16)
    for g in range(KV_GROUP):
        cols = slice(g * HEAD_DIM, (g + 1) * HEAD_DIM)
        q = (q_ref[:, cols] * Q_SCALE).astype(BF16)
        s = jnp.dot(q, kt, preferred_element_type=F32)
        sk = sink_ref[kh * KV_GROUP + g] * LOG2E
        m = jnp.maximum(jnp.max(_fold_lanes(s, jnp.maximum), axis=-1, keepdims=True), sk)
        p = jnp.exp2(s - m)
        denom = jnp.sum(_fold_lanes(p, jnp.add), axis=-1, keepdims=True) + jnp.exp2(sk - m)
        o = jnp.dot(p.astype(BF16), v, preferred_element_type=F32)
        o_ref[:, cols] = (o / denom).astype(o_ref.dtype)


def _fold_lanes(x, op):
    tiles = [x[:, t * HEAD_DIM:(t + 1) * HEAD_DIM] for t in range(x.shape[1] // HEAD_DIM)]
    while len(tiles) > 1:
        tiles = [op(tiles[t], tiles[t + 1]) for t in range(0, len(tiles), 2)]
    return tiles[0]


def _rope(x, cos, sin_lo, sin_hi):
    return x * cos + pltpu.roll(x, 96, 1) * sin_lo + pltpu.roll(x, 32, 1) * sin_hi


def _attn_lat_kernel(sink_ref, q_ref, k_ref, v_ref, ck_ref, cv_ref, cos_ref, slo_ref, shi_ref,
                     cosq_ref, sloq_ref, shiq_ref, o_ref, kt_ref, vb_ref, ckt_ref, cvb_ref):
    kh = pl.program_id(1)
    n_blocks = LAT_LEN // ATTN_BLOCK
    for n in range(n_blocks):
        blk = slice(n * ATTN_BLOCK, (n + 1) * ATTN_BLOCK)
        kt_ref[n] = _rope(k_ref[blk, :], cos_ref[blk, :], slo_ref[blk, :], shi_ref[blk, :]).T.astype(BF16)
    vb_ref[...] = v_ref[...].astype(BF16)
    ckt_ref[...] = ck_ref[...].T.astype(BF16)
    cvb_ref[...] = cv_ref[...].astype(BF16)

    rows = KV_GROUP * ATTN_BLOCK
    kq = (lax.broadcasted_iota(jnp.int32, (rows, ATTN_BLOCK), 1)
          - lax.broadcasted_iota(jnp.int32, (rows, ATTN_BLOCK), 0) % ATTN_BLOCK)
    sk = jnp.concatenate(
        [jnp.full((ATTN_BLOCK, 1), sink_ref[kh * KV_GROUP + g] * LOG2E, F32) for g in range(KV_GROUP)],
        axis=0)

    def scores(n):
        r0 = pl.multiple_of(n * ATTN_BLOCK, ATTN_BLOCK)
        cos = cosq_ref[pl.ds(r0, ATTN_BLOCK), :]
        slo = sloq_ref[pl.ds(r0, ATTN_BLOCK), :]
        shi = shiq_ref[pl.ds(r0, ATTN_BLOCK), :]
        q = jnp.concatenate(
            [_rope(q_ref[pl.ds(r0, ATTN_BLOCK), g * HEAD_DIM:(g + 1) * HEAD_DIM], cos, slo, shi).astype(BF16)
             for g in range(KV_GROUP)], axis=0)
        n_prev = jnp.maximum(n - 1, 0)
        n_next = jnp.minimum(n + 1, n_blocks - 1)
        thr_p = jnp.where(n >= 1, 0, 2 * ATTN_BLOCK)
        thr_n = jnp.where(n <= n_blocks - 2, 0, -2 * ATTN_BLOCK)
        s_p = jnp.where(kq >= thr_p, jnp.dot(q, kt_ref[n_prev], preferred_element_type=F32), NEG)
        s_0 = jnp.dot(q, kt_ref[n], preferred_element_type=F32)
        s_n = jnp.where(kq <= thr_n, jnp.dot(q, kt_ref[n_next], preferred_element_type=F32), NEG)
        s_c = jnp.dot(q, ckt_ref[...], preferred_element_type=F32)
        return s_p, s_0, s_n, s_c

    def softmax(s_p, s_0, s_n, s_c):
        m = jnp.maximum(jnp.maximum(s_p, s_0), jnp.maximum(s_n, _fold_lanes(s_c, jnp.maximum)))
        m = jnp.maximum(jnp.max(m, axis=-1, keepdims=True), sk)
        p_p = jnp.exp2(s_p - m)
        p_0 = jnp.exp2(s_0 - m)
        p_n = jnp.exp2(s_n - m)
        p_c = jnp.exp2(s_c - m)
        denom = (jnp.sum((p_p + p_0) + (p_n + _fold_lanes(p_c, jnp.add)), axis=-1, keepdims=True)
                 + jnp.exp2(sk - m))
        return p_p.astype(BF16), p_0.astype(BF16), p_n.astype(BF16), p_c.astype(BF16), denom

    def weighted_values(n, p_p, p_0, p_n, p_c, denom):
        r0 = pl.multiple_of(n * ATTN_BLOCK, ATTN_BLOCK)
        rp = pl.multiple_of(jnp.maximum(n - 1, 0) * ATTN_BLOCK, ATTN_BLOCK)
        rn = pl.multiple_of(jnp.minimum(n + 1, n_blocks - 1) * ATTN_BLOCK, ATTN_BLOCK)
        o = (jnp.dot(p_p, vb_ref[pl.ds(rp, ATTN_BLOCK), :], preferred_element_type=F32)
             + jnp.dot(p_0, vb_ref[pl.ds(r0, ATTN_BLOCK), :], preferred_element_type=F32)
             + jnp.dot(p_n, vb_ref[pl.ds(rn, ATTN_BLOCK), :], preferred_element_type=F32)
             + jnp.dot(p_c, cvb_ref[...], preferred_element_type=F32))
        o = o / denom
        for g in range(KV_GROUP):
            o_ref[pl.ds(r0, ATTN_BLOCK), g * HEAD_DIM:(g + 1) * HEAD_DIM] = (
                o[g * ATTN_BLOCK:(g + 1) * ATTN_BLOCK, :].astype(o_ref.dtype))

    def group(i, carry):
        blocks = [i * ATTN_GROUP + j for j in range(ATTN_GROUP)]
        s = [scores(n) for n in blocks]
        p = [softmax(*sn) for sn in s]
        for n, pn in zip(blocks, p):
            weighted_values(n, *pn)
        return carry

    lax.fori_loop(0, n_blocks // ATTN_GROUP, group, 0)


def _rope_tables():
    pos = jnp.arange(LAT_LEN, dtype=jnp.int32)
    row = (pos // GRID_W).astype(F32)
    col = (pos % GRID_W).astype(F32)
    half = HEAD_DIM // 2
    inv_freq = ROPE_BASE ** (-jnp.arange(0, half, 2, dtype=F32) / half)
    ang_r = row[:, None] * inv_freq[None, :]
    ang_c = col[:, None] * inv_freq[None, :]
    zero = jnp.zeros_like(ang_r)
    cos = jnp.concatenate([jnp.cos(ang_r), jnp.cos(ang_r), jnp.cos(ang_c), jnp.cos(ang_c)], axis=1)
    sin_lo = jnp.concatenate([-jnp.sin(ang_r), zero, -jnp.sin(ang_c), zero], axis=1)
    sin_hi = jnp.concatenate([zero, jnp.sin(ang_r), zero, jnp.sin(ang_c)], axis=1)
    return cos, sin_lo, sin_hi


def _attention(qkv, cache_k, cache_v, sink, layer_a):
    q_cols = KV_GROUP * HEAD_DIM
    k_blk0 = N_HEADS
    v_blk0 = N_HEADS + N_KV_HEADS
    ctx_rows0 = T_CTX // LAT_LEN
    cos, sin_lo, sin_hi = _rope_tables()
    q_tabs = [t * Q_SCALE for t in (cos, sin_lo, sin_hi)]
    ck = cache_k.reshape(N_LAT_REQ, -1, PAST_LEN, N_KV_HEADS * HEAD_DIM)
    cv = cache_v.reshape(N_LAT_REQ, -1, PAST_LEN, N_KV_HEADS * HEAD_DIM)
    tab_spec = pl.BlockSpec((LAT_LEN, HEAD_DIM), lambda b, h: (0, 0))
    smem_spec = pl.BlockSpec(memory_space=pltpu.SMEM)
    lat_block = LAT_LEN * HEAD_DIM * 4
    vmem_lat = (2 * (q_cols // HEAD_DIM + 2 + 1 + 6) * lat_block + 2 * LAT_LEN * q_cols * 2
                + 3 * lat_block + 16 * MIB)
    o_lat = pl.pallas_call(
        _attn_lat_kernel,
        grid=(N_LAT_REQ, N_KV_HEADS),
        in_specs=[
            smem_spec,
            pl.BlockSpec((LAT_LEN, q_cols), lambda b, h: (ctx_rows0 + b, h)),
            pl.BlockSpec((LAT_LEN, HEAD_DIM), lambda b, h: (ctx_rows0 + b, k_blk0 + h)),
            pl.BlockSpec((LAT_LEN, HEAD_DIM), lambda b, h: (ctx_rows0 + b, v_blk0 + h)),
            pl.BlockSpec((None, None, PAST_LEN, HEAD_DIM), lambda b, h: (b, layer_a, 0, h)),
            pl.BlockSpec((None, None, PAST_LEN, HEAD_DIM), lambda b, h: (b, layer_a, 0, h)),
            tab_spec, tab_spec, tab_spec, tab_spec, tab_spec, tab_spec,
        ],
        out_specs=pl.BlockSpec((LAT_LEN, q_cols), lambda b, h: (ctx_rows0 + b, h)),
        out_shape=jax.ShapeDtypeStruct((T_ALL, N_HEADS * HEAD_DIM), BF16),
        scratch_shapes=[pltpu.VMEM((LAT_LEN // ATTN_BLOCK, HEAD_DIM, ATTN_BLOCK), BF16),
                        pltpu.VMEM((LAT_LEN, HEAD_DIM), BF16),
                        pltpu.VMEM((HEAD_DIM, PAST_LEN), BF16), pltpu.VMEM((PAST_LEN, HEAD_DIM), BF16)],
        compiler_params=_params(vmem_lat, 2),
        name="attn_latent",
    )(sink, qkv, qkv, qkv, ck, cv, cos, sin_lo, sin_hi, *q_tabs)

    vmem_ctx = 2 * (CTX_LEN * q_cols * 4 + 2 * CTX_LEN * HEAD_DIM * 4 + CTX_LEN * q_cols * 2) + 8 * MIB
    return pl.pallas_call(
        _attn_ctx_kernel,
        grid=(N_CTX_REQ, N_KV_HEADS),
        in_specs=[
            smem_spec,
            pl.BlockSpec((CTX_LEN, q_cols), lambda b, h: (b, h)),
            pl.BlockSpec((CTX_LEN, HEAD_DIM), lambda b, h: (b, k_blk0 + h)),
            pl.BlockSpec((CTX_LEN, HEAD_DIM), lambda b, h: (b, v_blk0 + h)),
            pl.BlockSpec(memory_space=pl.ANY),
        ],
        out_specs=pl.BlockSpec((CTX_LEN, q_cols), lambda b, h: (b, h)),
        out_shape=jax.ShapeDtypeStruct((T_ALL, N_HEADS * HEAD_DIM), BF16),
        input_output_aliases={4: 0},
        compiler_params=_params(vmem_ctx, 2),
        name="attn_context",
    )(sink, qkv, qkv, qkv, o_lat)


def _dft_tables(n):
    k = jnp.arange(n, dtype=jnp.int32)
    ang = ((k[:, None] * k[None, :]) % n).astype(F32) * (2.0 * math.pi / n)
    return jnp.cos(ang), jnp.sin(ang)


def _fourier_group(x, cs_ref, pos_ref, norm):
    z = jnp.dot(x, cs_ref[...], preferred_element_type=F32)
    zz = jnp.concatenate([z[:, :GROUP_DIM], z[:, GROUP_DIM:]], axis=0).astype(BF16)
    return jnp.dot(pos_ref[...], zz, preferred_element_type=F32) * norm


def _fourier_lat_kernel(x_ref, cs_ref, pos_ref, o_ref):
    norm = (LAT_LEN * GROUP_DIM) ** -0.5
    o_ref[...] = _fourier_group(x_ref[...], cs_ref, pos_ref, norm).astype(o_ref.dtype)


def _fourier_ctx_kernel(x_ref, cs_ref, pos_ref, prev_ref, o_ref):
    del prev_ref
    norm = (CTX_LEN * GROUP_DIM) ** -0.5
    for g in range(N_GROUPS):
        cols = slice(g * GROUP_DIM, (g + 1) * GROUP_DIM)
        o_ref[:, cols] = _fourier_group(x_ref[:, cols], cs_ref, pos_ref, norm).astype(o_ref.dtype)


def _fourier(h):
    cc, sc = _dft_tables(GROUP_DIM)
    cs = jnp.concatenate([cc, sc], axis=1).astype(BF16)
    cl, sl = _dft_tables(LAT_LEN)
    pos_lat = jnp.concatenate([cl, -sl], axis=1).astype(BF16)
    cp, sp = _dft_tables(CTX_LEN)
    pos_ctx = jnp.concatenate([cp, -sp], axis=1).astype(BF16)
    ctx_rows0 = T_CTX // LAT_LEN

    blk = LAT_LEN * GROUP_DIM
    vmem_lat = 2 * (2 * blk * 2 + GROUP_DIM * 2 * GROUP_DIM * 2 + LAT_LEN * 2 * LAT_LEN * 2) + 5 * blk * 4 + 4 * MIB
    f_lat = pl.pallas_call(
        _fourier_lat_kernel,
        grid=(N_LAT_REQ, N_GROUPS),
        in_specs=[pl.BlockSpec((LAT_LEN, GROUP_DIM), lambda b, g: (ctx_rows0 + b, g)),
                  pl.BlockSpec((GROUP_DIM, 2 * GROUP_DIM), lambda b, g: (0, 0)),
                  pl.BlockSpec((LAT_LEN, 2 * LAT_LEN), lambda b, g: (0, 0))],
        out_specs=pl.BlockSpec((LAT_LEN, GROUP_DIM), lambda b, g: (ctx_rows0 + b, g)),
        out_shape=jax.ShapeDtypeStruct((T_ALL, D_MODEL), BF16),
        compiler_params=_params(vmem_lat, 2),
        name="fourier_latent",
    )(h, cs, pos_lat)

    vmem_ctx = 2 * (2 * CTX_LEN * D_MODEL * 2 + GROUP_DIM * 2 * GROUP_DIM * 2 + CTX_LEN * 2 * CTX_LEN * 2) + 16 * MIB
    return pl.pallas_call(
        _fourier_ctx_kernel,
        grid=(N_CTX_REQ,),
        in_specs=[pl.BlockSpec((CTX_LEN, D_MODEL), lambda b: (b, 0)),
                  pl.BlockSpec((GROUP_DIM, 2 * GROUP_DIM), lambda b: (0, 0)),
                  pl.BlockSpec((CTX_LEN, 2 * CTX_LEN), lambda b: (0, 0)),
                  pl.BlockSpec(memory_space=pl.ANY)],
        out_specs=pl.BlockSpec((CTX_LEN, D_MODEL), lambda b: (b, 0)),
        out_shape=jax.ShapeDtypeStruct((T_ALL, D_MODEL), BF16),
        input_output_aliases={3: 0},
        compiler_params=_params(vmem_ctx, 1),
        name="fourier_context",
    )(h, cs, pos_ctx, f_lat)


def kernel(x_prompt, x_sample, cache_k, cache_v, c, c_ctx, w_ada, b_ada, g_mix_pre, g_mix_post,
           g_ffn_pre, g_ffn_post, w_qkv, w_attn_out, attn_sink, w_fourier, w_ffn_in, w_ffn_out):
    xp = x_prompt.reshape(T_CTX, D_MODEL)
    xs = x_sample.reshape(T_LAT, D_MODEL)
    cond = jnp.concatenate(
        [c, c_ctx[None, :], jnp.zeros((MOD_ROWS - N_LAT_REQ - 1, D_MODEL), F32)], axis=0)
    mod = _ada_table(cond, w_ada, b_ada).reshape(DEPTH, MOD_ROWS, 6, D_MODEL)
    g_mix_pre, g_mix_post, g_ffn_pre, g_ffn_post = (
        g.reshape(DEPTH, 1, D_MODEL) for g in (g_mix_pre, g_mix_post, g_ffn_pre, g_ffn_post))
    new_k, new_v = [], []
    y = (xp, xs)
    h = _pre_call(xp, xs, g_mix_pre, mod, 0, 0)
    for i in range(DEPTH):
        if i % 2 == 0:
            a = i // 2
            qkv = _project(h, w_qkv, a, rows=MM_ROWS, cols=1024, chunk_rows=512, out_dtype=F32,
                           name="qkv_proj")
            k0 = N_HEADS * HEAD_DIM
            k1 = k0 + N_KV_HEADS * HEAD_DIM
            new_k.append(qkv[:T_CTX, k0:k1].reshape(N_CTX_REQ, CTX_LEN, N_KV_HEADS, HEAD_DIM))
            new_v.append(qkv[:T_CTX, k1:].reshape(N_CTX_REQ, CTX_LEN, N_KV_HEADS, HEAD_DIM))
            att = _attention(qkv, cache_k, cache_v, attn_sink[a], a)
            o = _project(att, w_attn_out, a, rows=MM_ROWS, cols=1024, chunk_rows=512, out_dtype=BF16,
                         name="attn_out_proj")
        else:
            o = _project(_fourier(h), w_fourier, i // 2, rows=MM_ROWS, cols=1024, chunk_rows=512,
                         out_dtype=BF16, name="fourier_proj")
        y, h = _postpre_call(o, y, g_mix_post, mod, i, 0, g_ffn_pre, i, 1)
        act = _swiglu_in(h, w_ffn_in, i)
        o = _project(act, w_ffn_out, i, rows=FFN_OUT_ROWS, cols=512, chunk_rows=256, out_dtype=BF16,
                     name="ffn_out_proj")
        if i + 1 < DEPTH:
            y, h = _postpre_call(o, y, g_ffn_post, mod, i, 1, g_mix_pre, i + 1, 0)
        else:
            y_p, y_s = _post_call(o, y, g_ffn_post, mod, i, 1)

    return (y_p.reshape(N_CTX_REQ, CTX_LEN, D_MODEL), y_s.reshape(N_LAT_REQ, LAT_LEN, D_MODEL),
            jnp.stack(new_k, axis=1), jnp.stack(new_v, axis=1))
```

```python
import functools
import math

import jax
import jax.numpy as jnp
from jax import lax
from jax.experimental import pallas as pl
from jax.experimental.pallas import tpu as pltpu

F32 = jnp.float32
BF16 = jnp.bfloat16

D_MODEL = 4096
N_CTX_REQ = 16
CTX_LEN = 256
N_LAT_REQ = 8
LAT_LEN = 1024
PAST_LEN = 512
DEPTH = 4
GRID_W = 64
HEAD_DIM = 128
N_HEADS = 32
N_KV_HEADS = 8
KV_GROUP = 4
ATTN_BLOCK = 128
ATTN_GROUP = 4
ROPE_BASE = 10000.0
N_GROUPS = 8
GROUP_DIM = D_MODEL // N_GROUPS
D_FF = 11008
EPS = 1e-6
NEG = -1e30

T_CTX = N_CTX_REQ * CTX_LEN
T_LAT = N_LAT_REQ * LAT_LEN
T_ALL = T_CTX + T_LAT
MOD_ROWS = 16
CTX_MOD_ROW = N_LAT_REQ
QKV_W = (N_HEADS + 2 * N_KV_HEADS) * HEAD_DIM
ATTN_SCALE = HEAD_DIM ** -0.5
LOG2E = math.log2(math.e)
Q_SCALE = ATTN_SCALE * LOG2E

V7X_VMEM_BYTES = 64 * 1024 * 1024
MIB = 1024 * 1024

ELEM_ROWS = 256
MM_ROWS = 1024
FFN_IN_ROWS = 2048
FFN_OUT_ROWS = 512
FFN_TILE = 256


def _params(vmem_bytes, ngrid):
    assert vmem_bytes <= V7X_VMEM_BYTES
    return pltpu.CompilerParams(
        dimension_semantics=("arbitrary",) * ngrid,
        vmem_limit_bytes=int(vmem_bytes),
    )


def _mod_row_of_block(i, rows_per_block):
    ctx_blocks = T_CTX // rows_per_block
    blocks_per_lat = LAT_LEN // rows_per_block
    return jnp.where(i < ctx_blocks, CTX_MOD_ROW, (i - ctx_blocks) // blocks_per_lat)


ADA_TILE = 512


def _ada_kernel(c_ref, w_ref, b_ref, o_ref):
    c = c_ref[...]
    s = (c * jax.nn.sigmoid(c)).astype(BF16)
    w = w_ref[0].astype(BF16)
    o_ref[0] = jnp.dot(s, w, preferred_element_type=F32) + b_ref[0]


def _ada_table(cond, w_ada, b_ada):
    n_out = 6 * D_MODEL
    vmem = 2 * (D_MODEL * ADA_TILE * 4) + D_MODEL * ADA_TILE * 2 + 4 * MIB
    return pl.pallas_call(
        _ada_kernel,
        grid=(DEPTH, n_out // ADA_TILE),
        in_specs=[
            pl.BlockSpec((MOD_ROWS, D_MODEL), lambda l, j: (0, 0)),
            pl.BlockSpec((1, D_MODEL, ADA_TILE), lambda l, j: (l, 0, j)),
            pl.BlockSpec((1, 1, ADA_TILE), lambda l, j: (l, 0, j)),
        ],
        out_specs=pl.BlockSpec((1, MOD_ROWS, ADA_TILE), lambda l, j: (l, 0, j)),
        out_shape=jax.ShapeDtypeStruct((DEPTH, MOD_ROWS, n_out), F32),
        compiler_params=_params(vmem, 2),
        name="ada_table",
    )(cond, w_ada, b_ada.reshape(DEPTH, 1, n_out))


def _rms(x, g):
    return x * lax.rsqrt(jnp.mean(x * x, axis=-1, keepdims=True) + EPS) * g


def _pre(y, g_pre, mod_ref, sub):
    shift = mod_ref[pl.ds(3 * sub, 1), :]
    scale = mod_ref[pl.ds(3 * sub + 1, 1), :]
    return _rms(y, g_pre) * (1.0 + scale) + shift


_CTX_BLOCKS = T_CTX // ELEM_ROWS


def _read_stream(yp_ref, ys_ref):
    return jnp.where(pl.program_id(0) < _CTX_BLOCKS, yp_ref[...], ys_ref[...])


def _pre_kernel(sub, yp_ref, ys_ref, gpre_ref, mod_ref, h_ref):
    h_ref[...] = _pre(_read_stream(yp_ref, ys_ref), gpre_ref[...], mod_ref, sub).astype(h_ref.dtype)


def _post_kernel(sub, o_ref, y_ref, gpost_ref, mod_ref, yp_ref, ys_ref):
    gate = mod_ref[pl.ds(3 * sub + 2, 1), :]
    yn = y_ref[...] + gate * _rms(o_ref[...].astype(F32), gpost_ref[...])

    @pl.when(pl.program_id(0) < _CTX_BLOCKS)
    def _():
        yp_ref[...] = yn

    @pl.when(pl.program_id(0) >= _CTX_BLOCKS)
    def _():
        ys_ref[...] = yn


def _postpre_kernel(sub, nsub, split_in, o_ref, *refs):
    if split_in:
        yp_ref, ys_ref, gpost_ref, mod_ref, gpre_ref, nmod_ref, yo_ref, h_ref = refs
        y = _read_stream(yp_ref, ys_ref)
    else:
        y_ref, gpost_ref, mod_ref, gpre_ref, nmod_ref, yo_ref, h_ref = refs
        y = y_ref[...]
    gate = mod_ref[pl.ds(3 * sub + 2, 1), :]
    yn = y + gate * _rms(o_ref[...].astype(F32), gpost_ref[...])
    yo_ref[...] = yn
    h_ref[...] = _pre(yn, gpre_ref[...], nmod_ref, nsub).astype(h_ref.dtype)


def _row_spec():
    return pl.BlockSpec((ELEM_ROWS, D_MODEL), lambda i: (i, 0))


def _split_specs():
    return [pl.BlockSpec((ELEM_ROWS, D_MODEL), lambda i: (jnp.minimum(i, _CTX_BLOCKS - 1), 0)),
            pl.BlockSpec((ELEM_ROWS, D_MODEL), lambda i: (jnp.maximum(i - _CTX_BLOCKS, 0), 0))]


def _gain_spec(layer):
    return pl.BlockSpec((None, 1, D_MODEL), lambda i: (layer, 0, 0))


def _mod_spec(layer):
    return pl.BlockSpec((None, None, 6, D_MODEL),
                        lambda i: (layer, _mod_row_of_block(i, ELEM_ROWS), 0, 0))


_ELEM_BLOCK_F32 = ELEM_ROWS * D_MODEL * 4


def _pre_call(xp, xs, g_pre, mod, layer, sub):
    vmem = 2 * (2 * _ELEM_BLOCK_F32 + _ELEM_BLOCK_F32 // 2) + 3 * _ELEM_BLOCK_F32 + 2 * MIB
    return pl.pallas_call(
        functools.partial(_pre_kernel, sub),
        grid=(T_ALL // ELEM_ROWS,),
        in_specs=_split_specs() + [_gain_spec(layer), _mod_spec(layer)],
        out_specs=_row_spec(),
        out_shape=jax.ShapeDtypeStruct((T_ALL, D_MODEL), BF16),
        compiler_params=_params(vmem, 1),
        name="pre_norm",
    )(xp, xs, g_pre, mod)


def _post_call(o, y, g_post, mod, layer, sub):
    vmem = 2 * 4 * _ELEM_BLOCK_F32 + 3 * _ELEM_BLOCK_F32 + 2 * MIB
    return pl.pallas_call(
        functools.partial(_post_kernel, sub),
        grid=(T_ALL // ELEM_ROWS,),
        in_specs=[_row_spec(), _row_spec(), _gain_spec(layer), _mod_spec(layer)],
        out_specs=_split_specs(),
        out_shape=[jax.ShapeDtypeStruct((T_CTX, D_MODEL), F32),
                   jax.ShapeDtypeStruct((T_LAT, D_MODEL), F32)],
        compiler_params=_params(vmem, 1),
        name="post_norm",
    )(o, y, g_post, mod)


def _postpre_call(o, y, g_post, mod, layer, sub, g_pre, nlayer, nsub):
    split_in = isinstance(y, tuple)
    y_args = y if split_in else (y,)
    y_specs = _split_specs() if split_in else [_row_spec()]
    vmem = 2 * ((2 + len(y_args)) * _ELEM_BLOCK_F32 + _ELEM_BLOCK_F32 // 2) + 4 * _ELEM_BLOCK_F32 + 2 * MIB
    return pl.pallas_call(
        functools.partial(_postpre_kernel, sub, nsub, split_in),
        grid=(T_ALL // ELEM_ROWS,),
        in_specs=[_row_spec()] + y_specs + [_gain_spec(layer), _mod_spec(layer),
                                            _gain_spec(nlayer), _mod_spec(nlayer)],
        out_specs=[_row_spec(), _row_spec()],
        out_shape=[jax.ShapeDtypeStruct((T_ALL, D_MODEL), F32),
                   jax.ShapeDtypeStruct((T_ALL, D_MODEL), BF16)],
        compiler_params=_params(vmem, 1),
        name="post_pre_norm",
    )(o, *y_args, g_post, mod, g_pre, mod)


STREAM_SLOTS = 2


def _aligned(v, m):
    return v if isinstance(v, int) else pl.multiple_of(v, m)


def _wstream_kernel(layer, seg_starts, seg_width, chunk_rows, n_chunks, epilogue,
                    x_ref, w_hbm, o_ref, wb0_ref, wb1_ref, stage_ref, sem):
    j = pl.program_id(0)
    i = pl.program_id(1)
    nj = pl.num_programs(0)
    ni = pl.num_programs(1)
    step = j * ni + i
    set_new = lax.rem(step, 2)
    set_old = 1 - set_new

    def chunk_copies(tile, c, set_, slot):
        r0 = _aligned(c * chunk_rows, chunk_rows)
        return [pltpu.make_async_copy(
            w_hbm.at[layer, pl.ds(r0, chunk_rows), pl.ds(_aligned(start(tile), seg_width), seg_width)],
            stage_ref.at[set_, slot, :, pl.ds(s * seg_width, seg_width)],
            sem.at[set_, slot]) for s, start in enumerate(seg_starts)]

    def group_chunk(group, slot):
        return jnp.minimum(group * STREAM_SLOTS + slot, n_chunks - 1)

    def start_group(tile, group, set_):
        for slot in range(STREAM_SLOTS):
            for cp in chunk_copies(tile, group_chunk(group, slot), set_, slot):
                cp.start()

    def round_chunk(wb_ref, c, set_, slot):
        wb_ref[pl.ds(_aligned(c * chunk_rows, chunk_rows), chunk_rows), :] = stage_ref[set_, slot].astype(BF16)

    @pl.when(step == 0)
    def _():
        for cp in chunk_copies(0, 0, 0, 0):
            cp.start()
        for c in range(n_chunks):
            if c + 1 < n_chunks:
                for cp in chunk_copies(0, c + 1, (c + 1) % 2, 0):
                    cp.start()
            for cp in chunk_copies(0, c, c % 2, 0):
                cp.wait()
            round_chunk(wb0_ref, c, c % 2, 0)
        start_group(jnp.minimum(1, nj - 1), 0, 1)

    for slot in range(STREAM_SLOTS):
        for cp in chunk_copies(0, 0, set_old, slot):
            cp.wait()

    @pl.when(step < nj * ni - 1)
    def _():
        wrap = i == ni - 1
        tile = jnp.minimum(jnp.where(wrap, j + 2, j + 1), nj - 1)
        start_group(tile, jnp.where(wrap, 0, i + 1), set_new)

    for parity, (wb_cur, wb_idle) in enumerate(((wb0_ref, wb1_ref), (wb1_ref, wb0_ref))):
        @pl.when(lax.rem(j, 2) == parity)
        def _(wb_cur=wb_cur, wb_idle=wb_idle):
            for slot in range(STREAM_SLOTS):
                round_chunk(wb_idle, group_chunk(i, slot), set_old, slot)
            acc = jnp.dot(x_ref[...], wb_cur[...], preferred_element_type=F32)
            o_ref[...] = epilogue(acc).astype(o_ref.dtype)


def _matmul_wstream(x, w_stack, layer, *, rows, seg_starts, seg_width, out_cols, chunk_rows, out_dtype,
                    epilogue, name):
    m, k = x.shape
    cols = len(seg_starts) * seg_width
    n_tiles = w_stack.shape[2] // cols
    n_row_blocks = m // rows
    n_chunks = k // chunk_rows
    assert m % rows == 0 and k % chunk_rows == 0 and w_stack.shape[2] % cols == 0
    assert n_row_blocks * STREAM_SLOTS >= n_chunks >= 2
    out_bytes = jnp.dtype(out_dtype).itemsize
    vmem = (2 * rows * k * 2 + 2 * k * cols * 2 + 2 * STREAM_SLOTS * chunk_rows * cols * 4
            + 2 * rows * out_cols * out_bytes + 2 * rows * cols * 4 + 2 * MIB)
    return pl.pallas_call(
        functools.partial(_wstream_kernel, layer, seg_starts, seg_width, chunk_rows, n_chunks, epilogue),
        grid=(n_tiles, n_row_blocks),
        in_specs=[pl.BlockSpec((rows, k), lambda j, i: (i, 0)),
                  pl.BlockSpec(memory_space=pl.ANY)],
        out_specs=pl.BlockSpec((rows, out_cols), lambda j, i: (i, j)),
        out_shape=jax.ShapeDtypeStruct((m, n_tiles * out_cols), out_dtype),
        scratch_shapes=[pltpu.VMEM((k, cols), BF16), pltpu.VMEM((k, cols), BF16),
                        pltpu.VMEM((2, STREAM_SLOTS, chunk_rows, cols), F32),
                        pltpu.SemaphoreType.DMA((2, STREAM_SLOTS))],
        compiler_params=_params(vmem, 2),
        name=name,
    )(x, w_stack)


def _identity(acc):
    return acc


def _project(x, w_stack, layer, *, rows, cols, chunk_rows, out_dtype, name):
    return _matmul_wstream(x, w_stack, layer, rows=rows, seg_starts=[lambda t: t * cols], seg_width=cols,
                           out_cols=cols, chunk_rows=chunk_rows, out_dtype=out_dtype, epilogue=_identity,
                           name=name)


def _swiglu(acc):
    g = acc[:, :FFN_TILE]
    u = acc[:, FFN_TILE:]
    return g * jax.nn.sigmoid(g) * u


def _swiglu_in(h, w_in_stack, layer):
    return _matmul_wstream(
        h, w_in_stack, layer, rows=FFN_IN_ROWS,
        seg_starts=[lambda t: t * FFN_TILE, lambda t: D_FF + t * FFN_TILE], seg_width=FFN_TILE,
        out_cols=FFN_TILE, chunk_rows=512, out_dtype=BF16, epilogue=_swiglu, name="ffn_in_swiglu")


def _attn_ctx_kernel(sink_ref, q_ref, k_ref, v_ref, prev_ref, o_ref):
    del prev_ref
    kh = pl.program_id(1)
    kt = k_ref[...].T.astype(BF16)
    v = v_ref[...].astype(BF16)
    for g in range(KV_GROUP):
        cols = slice(g * HEAD_DIM, (g + 1) * HEAD_DIM)
        q = (q_ref[:, cols] * Q_SCALE).astype(BF16)
        s = jnp.dot(q, kt, preferred_element_type=F32)
        sk = sink_ref[kh * KV_GROUP + g] * LOG2E
        m = jnp.maximum(jnp.max(_fold_lanes(s, jnp.maximum), axis=-1, keepdims=True), sk)
        p = jnp.exp2(s - m)
        denom = jnp.sum(_fold_lanes(p, jnp.add), axis=-1, keepdims=True) + jnp.exp2(sk - m)
        o = jnp.dot(p.astype(BF16), v, preferred_element_type=F32)
        o_ref[:, cols] = (o / denom).astype(o_ref.dtype)


def _fold_lanes(x, op):
    tiles = [x[:, t * HEAD_DIM:(t + 1) * HEAD_DIM] for t in range(x.shape[1] // HEAD_DIM)]
    while len(tiles) > 1:
        tiles = [op(tiles[t], tiles[t + 1]) for t in range(0, len(tiles), 2)]
    return tiles[0]


def _rope(x, cos, sin_lo, sin_hi):
    return x * cos + pltpu.roll(x, 96, 1) * sin_lo + pltpu.roll(x, 32, 1) * sin_hi


def _attn_lat_kernel(sink_ref, q_ref, k_ref, v_ref, ck_ref, cv_ref, cos_ref, slo_ref, shi_ref,
                     cosq_ref, sloq_ref, shiq_ref, o_ref, kt_ref, vb_ref, ckt_ref, cvb_ref):
    kh = pl.program_id(1)
    n_blocks = LAT_LEN // ATTN_BLOCK
    for n in range(n_blocks):
        blk = slice(n * ATTN_BLOCK, (n + 1) * ATTN_BLOCK)
        kt_ref[n] = _rope(k_ref[blk, :], cos_ref[blk, :], slo_ref[blk, :], shi_ref[blk, :]).T.astype(BF16)
    vb_ref[...] = v_ref[...].astype(BF16)
    ckt_ref[...] = ck_ref[...].T.astype(BF16)
    cvb_ref[...] = cv_ref[...].astype(BF16)

    rows = KV_GROUP * ATTN_BLOCK
    kq = (lax.broadcasted_iota(jnp.int32, (rows, ATTN_BLOCK), 1)
          - lax.broadcasted_iota(jnp.int32, (rows, ATTN_BLOCK), 0) % ATTN_BLOCK)
    sk = jnp.concatenate(
        [jnp.full((ATTN_BLOCK, 1), sink_ref[kh * KV_GROUP + g] * LOG2E, F32) for g in range(KV_GROUP)],
        axis=0)

    def scores(n):
        r0 = pl.multiple_of(n * ATTN_BLOCK, ATTN_BLOCK)
        cos = cosq_ref[pl.ds(r0, ATTN_BLOCK), :]
        slo = sloq_ref[pl.ds(r0, ATTN_BLOCK), :]
        shi = shiq_ref[pl.ds(r0, ATTN_BLOCK), :]
        q = jnp.concatenate(
            [_rope(q_ref[pl.ds(r0, ATTN_BLOCK), g * HEAD_DIM:(g + 1) * HEAD_DIM], cos, slo, shi).astype(BF16)
             for g in range(KV_GROUP)], axis=0)
        n_prev = jnp.maximum(n - 1, 0)
        n_next = jnp.minimum(n + 1, n_blocks - 1)
        thr_p = jnp.where(n >= 1, 0, 2 * ATTN_BLOCK)
        thr_n = jnp.where(n <= n_blocks - 2, 0, -2 * ATTN_BLOCK)
        s_p = jnp.where(kq >= thr_p, jnp.dot(q, kt_ref[n_prev], preferred_element_type=F32), NEG)
        s_0 = jnp.dot(q, kt_ref[n], preferred_element_type=F32)
        s_n = jnp.where(kq <= thr_n, jnp.dot(q, kt_ref[n_next], preferred_element_type=F32), NEG)
        s_c = jnp.dot(q, ckt_ref[...], preferred_element_type=F32)
        return s_p, s_0, s_n, s_c

    def softmax(s_p, s_0, s_n, s_c):
        m = jnp.maximum(jnp.maximum(s_p, s_0), jnp.maximum(s_n, _fold_lanes(s_c, jnp.maximum)))
        m = jnp.maximum(jnp.max(m, axis=-1, keepdims=True), sk)
        p_p = jnp.exp2(s_p - m)
        p_0 = jnp.exp2(s_0 - m)
        p_n = jnp.exp2(s_n - m)
        p_c = jnp.exp2(s_c - m)
        denom = (jnp.sum((p_p + p_0) + (p_n + _fold_lanes(p_c, jnp.add)), axis=-1, keepdims=True)
                 + jnp.exp2(sk - m))
        return p_p.astype(BF16), p_0.astype(BF16), p_n.astype(BF16), p_c.astype(BF16), denom

    def weighted_values(n, p_p, p_0, p_n, p_c, denom):
        r0 = pl.multiple_of(n * ATTN_BLOCK, ATTN_BLOCK)
        rp = pl.multiple_of(jnp.maximum(n - 1, 0) * ATTN_BLOCK, ATTN_BLOCK)
        rn = pl.multiple_of(jnp.minimum(n + 1, n_blocks - 1) * ATTN_BLOCK, ATTN_BLOCK)
        o = (jnp.dot(p_p, vb_ref[pl.ds(rp, ATTN_BLOCK), :], preferred_element_type=F32)
             + jnp.dot(p_0, vb_ref[pl.ds(r0, ATTN_BLOCK), :], preferred_element_type=F32)
             + jnp.dot(p_n, vb_ref[pl.ds(rn, ATTN_BLOCK), :], preferred_element_type=F32)
             + jnp.dot(p_c, cvb_ref[...], preferred_element_type=F32))
        o = o / denom
        for g in range(KV_GROUP):
            o_ref[pl.ds(r0, ATTN_BLOCK), g * HEAD_DIM:(g + 1) * HEAD_DIM] = (
                o[g * ATTN_BLOCK:(g + 1) * ATTN_BLOCK, :].astype(o_ref.dtype))

    def group(i, carry):
        blocks = [i * ATTN_GROUP + j for j in range(ATTN_GROUP)]
        s = [scores(n) for n in blocks]
        p = [softmax(*sn) for sn in s]
        for n, pn in zip(blocks, p):
            weighted_values(n, *pn)
        return carry

    lax.fori_loop(0, n_blocks // ATTN_GROUP, group, 0)


def _rope_tables():
    pos = jnp.arange(LAT_LEN, dtype=jnp.int32)
    row = (pos // GRID_W).astype(F32)
    col = (pos % GRID_W).astype(F32)
    half = HEAD_DIM // 2
    inv_freq = ROPE_BASE ** (-jnp.arange(0, half, 2, dtype=F32) / half)
    ang_r = row[:, None] * inv_freq[None, :]
    ang_c = col[:, None] * inv_freq[None, :]
    zero = jnp.zeros_like(ang_r)
    cos = jnp.concatenate([jnp.cos(ang_r), jnp.cos(ang_r), jnp.cos(ang_c), jnp.cos(ang_c)], axis=1)
    sin_lo = jnp.concatenate([-jnp.sin(ang_r), zero, -jnp.sin(ang_c), zero], axis=1)
    sin_hi = jnp.concatenate([zero, jnp.sin(ang_r), zero, jnp.sin(ang_c)], axis=1)
    return cos, sin_lo, sin_hi


def _attention(qkv, cache_k, cache_v, sink, layer_a):
    q_cols = KV_GROUP * HEAD_DIM
    k_blk0 = N_HEADS
    v_blk0 = N_HEADS + N_KV_HEADS
    ctx_rows0 = T_CTX // LAT_LEN
    cos, sin_lo, sin_hi = _rope_tables()
    q_tabs = [t * Q_SCALE for t in (cos, sin_lo, sin_hi)]
    ck = cache_k.reshape(N_LAT_REQ, -1, PAST_LEN, N_KV_HEADS * HEAD_DIM)
    cv = cache_v.reshape(N_LAT_REQ, -1, PAST_LEN, N_KV_HEADS * HEAD_DIM)
    tab_spec = pl.BlockSpec((LAT_LEN, HEAD_DIM), lambda b, h: (0, 0))
    smem_spec = pl.BlockSpec(memory_space=pltpu.SMEM)
    lat_block = LAT_LEN * HEAD_DIM * 4
    vmem_lat = (2 * (q_cols // HEAD_DIM + 2 + 1 + 6) * lat_block + 2 * LAT_LEN * q_cols * 2
                + 3 * lat_block + 16 * MIB)
    o_lat = pl.pallas_call(
        _attn_lat_kernel,
        grid=(N_LAT_REQ, N_KV_HEADS),
        in_specs=[
            smem_spec,
            pl.BlockSpec((LAT_LEN, q_cols), lambda b, h: (ctx_rows0 + b, h)),
            pl.BlockSpec((LAT_LEN, HEAD_DIM), lambda b, h: (ctx_rows0 + b, k_blk0 + h)),
            pl.BlockSpec((LAT_LEN, HEAD_DIM), lambda b, h: (ctx_rows0 + b, v_blk0 + h)),
            pl.BlockSpec((None, None, PAST_LEN, HEAD_DIM), lambda b, h: (b, layer_a, 0, h)),
            pl.BlockSpec((None, None, PAST_LEN, HEAD_DIM), lambda b, h: (b, layer_a, 0, h)),
            tab_spec, tab_spec, tab_spec, tab_spec, tab_spec, tab_spec,
        ],
        out_specs=pl.BlockSpec((LAT_LEN, q_cols), lambda b, h: (ctx_rows0 + b, h)),
        out_shape=jax.ShapeDtypeStruct((T_ALL, N_HEADS * HEAD_DIM), BF16),
        scratch_shapes=[pltpu.VMEM((LAT_LEN // ATTN_BLOCK, HEAD_DIM, ATTN_BLOCK), BF16),
                        pltpu.VMEM((LAT_LEN, HEAD_DIM), BF16),
                        pltpu.VMEM((HEAD_DIM, PAST_LEN), BF16), pltpu.VMEM((PAST_LEN, HEAD_DIM), BF16)],
        compiler_params=_params(vmem_lat, 2),
        name="attn_latent",
    )(sink, qkv, qkv, qkv, ck, cv, cos, sin_lo, sin_hi, *q_tabs)

    vmem_ctx = 2 * (CTX_LEN * q_cols * 4 + 2 * CTX_LEN * HEAD_DIM * 4 + CTX_LEN * q_cols * 2) + 8 * MIB
    return pl.pallas_call(
        _attn_ctx_kernel,
        grid=(N_CTX_REQ, N_KV_HEADS),
        in_specs=[
            smem_spec,
            pl.BlockSpec((CTX_LEN, q_cols), lambda b, h: (b, h)),
            pl.BlockSpec((CTX_LEN, HEAD_DIM), lambda b, h: (b, k_blk0 + h)),
            pl.BlockSpec((CTX_LEN, HEAD_DIM), lambda b, h: (b, v_blk0 + h)),
            pl.BlockSpec(memory_space=pl.ANY),
        ],
        out_specs=pl.BlockSpec((CTX_LEN, q_cols), lambda b, h: (b, h)),
        out_shape=jax.ShapeDtypeStruct((T_ALL, N_HEADS * HEAD_DIM), BF16),
        input_output_aliases={4: 0},
        compiler_params=_params(vmem_ctx, 2),
        name="attn_context",
    )(sink, qkv, qkv, qkv, o_lat)


def _dft_tables(n):
    k = jnp.arange(n, dtype=jnp.int32)
    ang = ((k[:, None] * k[None, :]) % n).astype(F32) * (2.0 * math.pi / n)
    return jnp.cos(ang), jnp.sin(ang)


def _fourier_group(x, cs_ref, pos_ref, norm):
    z = jnp.dot(x, cs_ref[...], preferred_element_type=F32)
    zz = jnp.concatenate([z[:, :GROUP_DIM], z[:, GROUP_DIM:]], axis=0).astype(BF16)
    return jnp.dot(pos_ref[...], zz, preferred_element_type=F32) * norm


def _fourier_lat_kernel(x_ref, cs_ref, pos_ref, o_ref):
    norm = (LAT_LEN * GROUP_DIM) ** -0.5
    o_ref[...] = _fourier_group(x_ref[...], cs_ref, pos_ref, norm).astype(o_ref.dtype)


def _fourier_ctx_kernel(x_ref, cs_ref, pos_ref, prev_ref, o_ref):
    del prev_ref
    norm = (CTX_LEN * GROUP_DIM) ** -0.5
    for g in range(N_GROUPS):
        cols = slice(g * GROUP_DIM, (g + 1) * GROUP_DIM)
        o_ref[:, cols] = _fourier_group(x_ref[:, cols], cs_ref, pos_ref, norm).astype(o_ref.dtype)


def _fourier(h):
    cc, sc = _dft_tables(GROUP_DIM)
    cs = jnp.concatenate([cc, sc], axis=1).astype(BF16)
    cl, sl = _dft_tables(LAT_LEN)
    pos_lat = jnp.concatenate([cl, -sl], axis=1).astype(BF16)
    cp, sp = _dft_tables(CTX_LEN)
    pos_ctx = jnp.concatenate([cp, -sp], axis=1).astype(BF16)
    ctx_rows0 = T_CTX // LAT_LEN

    blk = LAT_LEN * GROUP_DIM
    vmem_lat = 2 * (2 * blk * 2 + GROUP_DIM * 2 * GROUP_DIM * 2 + LAT_LEN * 2 * LAT_LEN * 2) + 5 * blk * 4 + 4 * MIB
    f_lat = pl.pallas_call(
        _fourier_lat_kernel,
        grid=(N_LAT_REQ, N_GROUPS),
        in_specs=[pl.BlockSpec((LAT_LEN, GROUP_DIM), lambda b, g: (ctx_rows0 + b, g)),
                  pl.BlockSpec((GROUP_DIM, 2 * GROUP_DIM), lambda b, g: (0, 0)),
                  pl.BlockSpec((LAT_LEN, 2 * LAT_LEN), lambda b, g: (0, 0))],
        out_specs=pl.BlockSpec((LAT_LEN, GROUP_DIM), lambda b, g: (ctx_rows0 + b, g)),
        out_shape=jax.ShapeDtypeStruct((T_ALL, D_MODEL), BF16),
        compiler_params=_params(vmem_lat, 2),
        name="fourier_latent",
    )(h, cs, pos_lat)

    vmem_ctx = 2 * (2 * CTX_LEN * D_MODEL * 2 + GROUP_DIM * 2 * GROUP_DIM * 2 + CTX_LEN * 2 * CTX_LEN * 2) + 16 * MIB
    return pl.pallas_call(
        _fourier_ctx_kernel,
        grid=(N_CTX_REQ,),
        in_specs=[pl.BlockSpec((CTX_LEN, D_MODEL), lambda b: (b, 0)),
                  pl.BlockSpec((GROUP_DIM, 2 * GROUP_DIM), lambda b: (0, 0)),
                  pl.BlockSpec((CTX_LEN, 2 * CTX_LEN), lambda b: (0, 0)),
                  pl.BlockSpec(memory_space=pl.ANY)],
        out_specs=pl.BlockSpec((CTX_LEN, D_MODEL), lambda b: (b, 0)),
        out_shape=jax.ShapeDtypeStruct((T_ALL, D_MODEL), BF16),
        input_output_aliases={3: 0},
        compiler_params=_params(vmem_ctx, 1),
        name="fourier_context",
    )(h, cs, pos_ctx, f_lat)


def kernel(x_prompt, x_sample, cache_k, cache_v, c, c_ctx, w_ada, b_ada, g_mix_pre, g_mix_post,
           g_ffn_pre, g_ffn_post, w_qkv, w_attn_out, attn_sink, w_fourier, w_ffn_in, w_ffn_out):
    xp = x_prompt.reshape(T_CTX, D_MODEL)
    xs = x_sample.reshape(T_LAT, D_MODEL)
    cond = jnp.concatenate(
        [c, c_ctx[None, :], jnp.zeros((MOD_ROWS - N_LAT_REQ - 1, D_MODEL), F32)], axis=0)
    mod = _ada_table(cond, w_ada, b_ada).reshape(DEPTH, MOD_ROWS, 6, D_MODEL)
    g_mix_pre, g_mix_post, g_ffn_pre, g_ffn_post = (
        g.reshape(DEPTH, 1, D_MODEL) for g in (g_mix_pre, g_mix_post, g_ffn_pre, g_ffn_post))
    new_k, new_v = [], []
    y = (xp, xs)
    h = _pre_call(xp, xs, g_mix_pre, mod, 0, 0)
    for i in range(DEPTH):
        if i % 2 == 0:
            a = i // 2
            qkv = _project(h, w_qkv, a, rows=MM_ROWS, cols=1024, chunk_rows=256, out_dtype=F32,
                           name="qkv_proj")
            k0 = N_HEADS * HEAD_DIM
            k1 = k0 + N_KV_HEADS * HEAD_DIM
            new_k.append(qkv[:T_CTX, k0:k1].reshape(N_CTX_REQ, CTX_LEN, N_KV_HEADS, HEAD_DIM))
            new_v.append(qkv[:T_CTX, k1:].reshape(N_CTX_REQ, CTX_LEN, N_KV_HEADS, HEAD_DIM))
            att = _attention(qkv, cache_k, cache_v, attn_sink[a], a)
            o = _project(att, w_attn_out, a, rows=MM_ROWS, cols=1024, chunk_rows=256, out_dtype=BF16,
                         name="attn_out_proj")
        else:
            o = _project(_fourier(h), w_fourier, i // 2, rows=MM_ROWS, cols=1024, chunk_rows=256,
                         out_dtype=BF16, name="fourier_proj")
        y, h = _postpre_call(o, y, g_mix_post, mod, i, 0, g_ffn_pre, i, 1)
        act = _swiglu_in(h, w_ffn_in, i)
        o = _project(act, w_ffn_out, i, rows=FFN_OUT_ROWS, cols=512, chunk_rows=256, out_dtype=BF16,
                     name="ffn_out_proj")
        if i + 1 < DEPTH:
            y, h = _postpre_call(o, y, g_ffn_post, mod, i, 1, g_mix_pre, i + 1, 0)
        else:
            y_p, y_s = _post_call(o, y, g_ffn_post, mod, i, 1)

    return (y_p.reshape(N_CTX_REQ, CTX_LEN, D_MODEL), y_s.reshape(N_LAT_REQ, LAT_LEN, D_MODEL),
            jnp.stack(new_k, axis=1), jnp.stack(new_v, axis=1))
```

```python
import functools
import math

import jax
import jax.numpy as jnp
from jax import lax
from jax.experimental import pallas as pl
from jax.experimental.pallas import tpu as pltpu

F32 = jnp.float32
BF16 = jnp.bfloat16

D_MODEL = 4096
N_CTX_REQ = 16
CTX_LEN = 256
N_LAT_REQ = 8
LAT_LEN = 1024
PAST_LEN = 512
DEPTH = 4
GRID_W = 64
HEAD_DIM = 128
N_HEADS = 32
N_KV_HEADS = 8
KV_GROUP = 4
ATTN_BLOCK = 128
ATTN_GROUP = 8
ROPE_BASE = 10000.0
N_GROUPS = 8
GROUP_DIM = D_MODEL // N_GROUPS
D_FF = 11008
EPS = 1e-6
NEG = -1e30

T_CTX = N_CTX_REQ * CTX_LEN
T_LAT = N_LAT_REQ * LAT_LEN
T_ALL = T_CTX + T_LAT
MOD_ROWS = 16
CTX_MOD_ROW = N_LAT_REQ
QKV_W = (N_HEADS + 2 * N_KV_HEADS) * HEAD_DIM
ATTN_SCALE = HEAD_DIM ** -0.5
LOG2E = math.log2(math.e)
Q_SCALE = ATTN_SCALE * LOG2E

V7X_VMEM_BYTES = 64 * 1024 * 1024
MIB = 1024 * 1024

ELEM_ROWS = 256
MM_ROWS = 1024
FFN_IN_ROWS = 2048
FFN_OUT_ROWS = 512
FFN_TILE = 256


def _params(vmem_bytes, ngrid):
    assert vmem_bytes <= V7X_VMEM_BYTES
    return pltpu.CompilerParams(
        dimension_semantics=("arbitrary",) * ngrid,
        vmem_limit_bytes=int(vmem_bytes),
    )


def _mod_row_of_block(i, rows_per_block):
    ctx_blocks = T_CTX // rows_per_block
    blocks_per_lat = LAT_LEN // rows_per_block
    return jnp.where(i < ctx_blocks, CTX_MOD_ROW, (i - ctx_blocks) // blocks_per_lat)


ADA_TILE = 512


def _ada_kernel(c_ref, w_ref, b_ref, o_ref):
    c = c_ref[...]
    s = (c * jax.nn.sigmoid(c)).astype(BF16)
    w = w_ref[0].astype(BF16)
    o_ref[0] = jnp.dot(s, w, preferred_element_type=F32) + b_ref[0]


def _ada_table(cond, w_ada, b_ada):
    n_out = 6 * D_MODEL
    vmem = 2 * (D_MODEL * ADA_TILE * 4) + D_MODEL * ADA_TILE * 2 + 4 * MIB
    return pl.pallas_call(
        _ada_kernel,
        grid=(DEPTH, n_out // ADA_TILE),
        in_specs=[
            pl.BlockSpec((MOD_ROWS, D_MODEL), lambda l, j: (0, 0)),
            pl.BlockSpec((1, D_MODEL, ADA_TILE), lambda l, j: (l, 0, j)),
            pl.BlockSpec((1, 1, ADA_TILE), lambda l, j: (l, 0, j)),
        ],
        out_specs=pl.BlockSpec((1, MOD_ROWS, ADA_TILE), lambda l, j: (l, 0, j)),
        out_shape=jax.ShapeDtypeStruct((DEPTH, MOD_ROWS, n_out), F32),
        compiler_params=_params(vmem, 2),
        name="ada_table",
    )(cond, w_ada, b_ada.reshape(DEPTH, 1, n_out))


def _rms(x, g):
    return x * lax.rsqrt(jnp.mean(x * x, axis=-1, keepdims=True) + EPS) * g


def _pre(y, g_pre, mod_ref, sub):
    shift = mod_ref[pl.ds(3 * sub, 1), :]
    scale = mod_ref[pl.ds(3 * sub + 1, 1), :]
    return _rms(y, g_pre) * (1.0 + scale) + shift


_CTX_BLOCKS = T_CTX // ELEM_ROWS


def _read_stream(yp_ref, ys_ref):
    return jnp.where(pl.program_id(0) < _CTX_BLOCKS, yp_ref[...], ys_ref[...])


def _pre_kernel(sub, yp_ref, ys_ref, gpre_ref, mod_ref, h_ref):
    h_ref[...] = _pre(_read_stream(yp_ref, ys_ref), gpre_ref[...], mod_ref, sub).astype(h_ref.dtype)


def _post_kernel(sub, o_ref, y_ref, gpost_ref, mod_ref, yp_ref, ys_ref):
    gate = mod_ref[pl.ds(3 * sub + 2, 1), :]
    yn = y_ref[...] + gate * _rms(o_ref[...].astype(F32), gpost_ref[...])

    @pl.when(pl.program_id(0) < _CTX_BLOCKS)
    def _():
        yp_ref[...] = yn

    @pl.when(pl.program_id(0) >= _CTX_BLOCKS)
    def _():
        ys_ref[...] = yn


def _postpre_kernel(sub, nsub, split_in, o_ref, *refs):
    if split_in:
        yp_ref, ys_ref, gpost_ref, mod_ref, gpre_ref, nmod_ref, yo_ref, h_ref = refs
        y = _read_stream(yp_ref, ys_ref)
    else:
        y_ref, gpost_ref, mod_ref, gpre_ref, nmod_ref, yo_ref, h_ref = refs
        y = y_ref[...]
    gate = mod_ref[pl.ds(3 * sub + 2, 1), :]
    yn = y + gate * _rms(o_ref[...].astype(F32), gpost_ref[...])
    yo_ref[...] = yn
    h_ref[...] = _pre(yn, gpre_ref[...], nmod_ref, nsub).astype(h_ref.dtype)


def _row_spec():
    return pl.BlockSpec((ELEM_ROWS, D_MODEL), lambda i: (i, 0))


def _split_specs():
    return [pl.BlockSpec((ELEM_ROWS, D_MODEL), lambda i: (jnp.minimum(i, _CTX_BLOCKS - 1), 0)),
            pl.BlockSpec((ELEM_ROWS, D_MODEL), lambda i: (jnp.maximum(i - _CTX_BLOCKS, 0), 0))]


def _gain_spec(layer):
    return pl.BlockSpec((None, 1, D_MODEL), lambda i: (layer, 0, 0))


def _mod_spec(layer):
    return pl.BlockSpec((None, None, 6, D_MODEL),
                        lambda i: (layer, _mod_row_of_block(i, ELEM_ROWS), 0, 0))


_ELEM_BLOCK_F32 = ELEM_ROWS * D_MODEL * 4


def _pre_call(xp, xs, g_pre, mod, layer, sub):
    vmem = 2 * (2 * _ELEM_BLOCK_F32 + _ELEM_BLOCK_F32 // 2) + 3 * _ELEM_BLOCK_F32 + 2 * MIB
    return pl.pallas_call(
        functools.partial(_pre_kernel, sub),
        grid=(T_ALL // ELEM_ROWS,),
        in_specs=_split_specs() + [_gain_spec(layer), _mod_spec(layer)],
        out_specs=_row_spec(),
        out_shape=jax.ShapeDtypeStruct((T_ALL, D_MODEL), BF16),
        compiler_params=_params(vmem, 1),
        name="pre_norm",
    )(xp, xs, g_pre, mod)


def _post_call(o, y, g_post, mod, layer, sub):
    vmem = 2 * 4 * _ELEM_BLOCK_F32 + 3 * _ELEM_BLOCK_F32 + 2 * MIB
    return pl.pallas_call(
        functools.partial(_post_kernel, sub),
        grid=(T_ALL // ELEM_ROWS,),
        in_specs=[_row_spec(), _row_spec(), _gain_spec(layer), _mod_spec(layer)],
        out_specs=_split_specs(),
        out_shape=[jax.ShapeDtypeStruct((T_CTX, D_MODEL), F32),
                   jax.ShapeDtypeStruct((T_LAT, D_MODEL), F32)],
        compiler_params=_params(vmem, 1),
        name="post_norm",
    )(o, y, g_post, mod)


def _postpre_call(o, y, g_post, mod, layer, sub, g_pre, nlayer, nsub):
    split_in = isinstance(y, tuple)
    y_args = y if split_in else (y,)
    y_specs = _split_specs() if split_in else [_row_spec()]
    vmem = 2 * ((2 + len(y_args)) * _ELEM_BLOCK_F32 + _ELEM_BLOCK_F32 // 2) + 4 * _ELEM_BLOCK_F32 + 2 * MIB
    return pl.pallas_call(
        functools.partial(_postpre_kernel, sub, nsub, split_in),
        grid=(T_ALL // ELEM_ROWS,),
        in_specs=[_row_spec()] + y_specs + [_gain_spec(layer), _mod_spec(layer),
                                            _gain_spec(nlayer), _mod_spec(nlayer)],
        out_specs=[_row_spec(), _row_spec()],
        out_shape=[jax.ShapeDtypeStruct((T_ALL, D_MODEL), F32),
                   jax.ShapeDtypeStruct((T_ALL, D_MODEL), BF16)],
        compiler_params=_params(vmem, 1),
        name="post_pre_norm",
    )(o, *y_args, g_post, mod, g_pre, mod)


STREAM_SLOTS = 2


def _aligned(v, m):
    return v if isinstance(v, int) else pl.multiple_of(v, m)


def _wstream_kernel(layer, seg_starts, seg_width, chunk_rows, n_chunks, epilogue,
                    x_ref, w_hbm, o_ref, wb0_ref, wb1_ref, stage_ref, sem):
    j = pl.program_id(0)
    i = pl.program_id(1)
    nj = pl.num_programs(0)
    ni = pl.num_programs(1)
    step = j * ni + i
    set_new = lax.rem(step, 2)
    set_old = 1 - set_new

    def chunk_copies(tile, c, set_, slot):
        r0 = _aligned(c * chunk_rows, chunk_rows)
        return [pltpu.make_async_copy(
            w_hbm.at[layer, pl.ds(r0, chunk_rows), pl.ds(_aligned(start(tile), seg_width), seg_width)],
            stage_ref.at[set_, slot, :, pl.ds(s * seg_width, seg_width)],
            sem.at[set_, slot]) for s, start in enumerate(seg_starts)]

    def group_chunk(group, slot):
        return jnp.minimum(group * STREAM_SLOTS + slot, n_chunks - 1)

    def start_group(tile, group, set_):
        for slot in range(STREAM_SLOTS):
            for cp in chunk_copies(tile, group_chunk(group, slot), set_, slot):
                cp.start()

    def round_chunk(wb_ref, c, set_, slot):
        wb_ref[pl.ds(_aligned(c * chunk_rows, chunk_rows), chunk_rows), :] = stage_ref[set_, slot].astype(BF16)

    @pl.when(step == 0)
    def _():
        for cp in chunk_copies(0, 0, 0, 0):
            cp.start()
        for c in range(n_chunks):
            if c + 1 < n_chunks:
                for cp in chunk_copies(0, c + 1, (c + 1) % 2, 0):
                    cp.start()
            for cp in chunk_copies(0, c, c % 2, 0):
                cp.wait()
            round_chunk(wb0_ref, c, c % 2, 0)
        start_group(jnp.minimum(1, nj - 1), 0, 1)

    for slot in range(STREAM_SLOTS):
        for cp in chunk_copies(0, 0, set_old, slot):
            cp.wait()

    @pl.when(step < nj * ni - 1)
    def _():
        wrap = i == ni - 1
        tile = jnp.minimum(jnp.where(wrap, j + 2, j + 1), nj - 1)
        start_group(tile, jnp.where(wrap, 0, i + 1), set_new)

    for parity, (wb_cur, wb_idle) in enumerate(((wb0_ref, wb1_ref), (wb1_ref, wb0_ref))):
        @pl.when(lax.rem(j, 2) == parity)
        def _(wb_cur=wb_cur, wb_idle=wb_idle):
            for slot in range(STREAM_SLOTS):
                round_chunk(wb_idle, group_chunk(i, slot), set_old, slot)
            acc = jnp.dot(x_ref[...], wb_cur[...], preferred_element_type=F32)
            o_ref[...] = epilogue(acc).astype(o_ref.dtype)


def _matmul_wstream(x, w_stack, layer, *, rows, seg_starts, seg_width, out_cols, chunk_rows, out_dtype,
                    epilogue, name):
    m, k = x.shape
    cols = len(seg_starts) * seg_width
    n_tiles = w_stack.shape[2] // cols
    n_row_blocks = m // rows
    n_chunks = k // chunk_rows
    assert m % rows == 0 and k % chunk_rows == 0 and w_stack.shape[2] % cols == 0
    assert n_row_blocks * STREAM_SLOTS >= n_chunks >= 2
    out_bytes = jnp.dtype(out_dtype).itemsize
    vmem = (2 * rows * k * 2 + 2 * k * cols * 2 + 2 * STREAM_SLOTS * chunk_rows * cols * 4
            + 2 * rows * out_cols * out_bytes + 2 * rows * cols * 4 + 2 * MIB)
    return pl.pallas_call(
        functools.partial(_wstream_kernel, layer, seg_starts, seg_width, chunk_rows, n_chunks, epilogue),
        grid=(n_tiles, n_row_blocks),
        in_specs=[pl.BlockSpec((rows, k), lambda j, i: (i, 0)),
                  pl.BlockSpec(memory_space=pl.ANY)],
        out_specs=pl.BlockSpec((rows, out_cols), lambda j, i: (i, j)),
        out_shape=jax.ShapeDtypeStruct((m, n_tiles * out_cols), out_dtype),
        scratch_shapes=[pltpu.VMEM((k, cols), BF16), pltpu.VMEM((k, cols), BF16),
                        pltpu.VMEM((2, STREAM_SLOTS, chunk_rows, cols), F32),
                        pltpu.SemaphoreType.DMA((2, STREAM_SLOTS))],
        compiler_params=_params(vmem, 2),
        name=name,
    )(x, w_stack)


def _identity(acc):
    return acc


def _project(x, w_stack, layer, *, rows, cols, chunk_rows, out_dtype, name):
    return _matmul_wstream(x, w_stack, layer, rows=rows, seg_starts=[lambda t: t * cols], seg_width=cols,
                           out_cols=cols, chunk_rows=chunk_rows, out_dtype=out_dtype, epilogue=_identity,
                           name=name)


def _swiglu(acc):
    g = acc[:, :FFN_TILE]
    u = acc[:, FFN_TILE:]
    return g * jax.nn.sigmoid(g) * u


def _swiglu_in(h, w_in_stack, layer):
    return _matmul_wstream(
        h, w_in_stack, layer, rows=FFN_IN_ROWS,
        seg_starts=[lambda t: t * FFN_TILE, lambda t: D_FF + t * FFN_TILE], seg_width=FFN_TILE,
        out_cols=FFN_TILE, chunk_rows=512, out_dtype=BF16, epilogue=_swiglu, name="ffn_in_swiglu")


def _attn_ctx_kernel(sink_ref, q_ref, k_ref, v_ref, prev_ref, o_ref):
    del prev_ref
    kh = pl.program_id(1)
    kt = k_ref[...].T.astype(BF16)
    v = v_ref[...].astype(BF16)
    heads = [slice(g * HEAD_DIM, (g + 1) * HEAD_DIM) for g in range(KV_GROUP)]
    scores = [jnp.dot((q_ref[:, cols] * Q_SCALE).astype(BF16), kt, preferred_element_type=F32)
              for cols in heads]
    probs = []
    for g, s in enumerate(scores):
        sk = sink_ref[kh * KV_GROUP + g] * LOG2E
        m = jnp.maximum(jnp.max(_fold_lanes(s, jnp.maximum), axis=-1, keepdims=True), sk)
        p = jnp.exp2(s - m)
        denom = jnp.sum(_fold_lanes(p, jnp.add), axis=-1, keepdims=True) + jnp.exp2(sk - m)
        probs.append((p.astype(BF16), denom))
    for cols, (p, denom) in zip(heads, probs):
        o = jnp.dot(p, v, preferred_element_type=F32)
        o_ref[:, cols] = (o / denom).astype(o_ref.dtype)


def _fold_lanes(x, op):
    tiles = [x[:, t * HEAD_DIM:(t + 1) * HEAD_DIM] for t in range(x.shape[1] // HEAD_DIM)]
    while len(tiles) > 1:
        tiles = [op(tiles[t], tiles[t + 1]) for t in range(0, len(tiles), 2)]
    return tiles[0]


def _rope(x, cos, sin_lo, sin_hi):
    return x * cos + pltpu.roll(x, 96, 1) * sin_lo + pltpu.roll(x, 32, 1) * sin_hi


def _attn_lat_kernel(sink_ref, q_ref, k_ref, v_ref, ck_ref, cv_ref, cos_ref, slo_ref, shi_ref,
                     cosq_ref, sloq_ref, shiq_ref, o_ref, kt_ref, vb_ref, ckt_ref, cvb_ref):
    kh = pl.program_id(1)
    n_blocks = LAT_LEN // ATTN_BLOCK
    kt = []
    for n in range(n_blocks):
        blk = slice(n * ATTN_BLOCK, (n + 1) * ATTN_BLOCK)
        kt.append(_rope(k_ref[blk, :], cos_ref[blk, :], slo_ref[blk, :], shi_ref[blk, :]).T.astype(BF16))
    pad = jnp.zeros((HEAD_DIM, ATTN_BLOCK), BF16)
    kt = [pad] + kt + [pad]
    for n in range(n_blocks):
        kt_ref[n] = jnp.concatenate(kt[n:n + 3], axis=1)
    vb_ref[:ATTN_BLOCK, :] = pad
    vb_ref[pl.ds(ATTN_BLOCK, LAT_LEN), :] = v_ref[...].astype(BF16)
    vb_ref[pl.ds(ATTN_BLOCK + LAT_LEN, ATTN_BLOCK), :] = pad
    ckt_ref[...] = ck_ref[...].T.astype(BF16)
    cvb_ref[...] = cv_ref[...].astype(BF16)

    rows = KV_GROUP * ATTN_BLOCK
    kq = (lax.broadcasted_iota(jnp.int32, (rows, ATTN_BLOCK), 1)
          - lax.broadcasted_iota(jnp.int32, (rows, ATTN_BLOCK), 0) % ATTN_BLOCK)
    sk = jnp.concatenate(
        [jnp.full((ATTN_BLOCK, 1), sink_ref[kh * KV_GROUP + g] * LOG2E, F32) for g in range(KV_GROUP)],
        axis=0)

    def scores(n):
        r0 = pl.multiple_of(n * ATTN_BLOCK, ATTN_BLOCK)
        cos = cosq_ref[pl.ds(r0, ATTN_BLOCK), :]
        slo = sloq_ref[pl.ds(r0, ATTN_BLOCK), :]
        shi = shiq_ref[pl.ds(r0, ATTN_BLOCK), :]
        q = jnp.concatenate(
            [_rope(q_ref[pl.ds(r0, ATTN_BLOCK), g * HEAD_DIM:(g + 1) * HEAD_DIM], cos, slo, shi).astype(BF16)
             for g in range(KV_GROUP)], axis=0)
        thr_p = jnp.where(n >= 1, 0, 2 * ATTN_BLOCK)
        thr_n = jnp.where(n <= n_blocks - 2, 0, -2 * ATTN_BLOCK)
        s_w = jnp.dot(q, kt_ref[n], preferred_element_type=F32)
        s_p = jnp.where(kq >= thr_p, s_w[:, :ATTN_BLOCK], NEG)
        s_0 = s_w[:, ATTN_BLOCK:2 * ATTN_BLOCK]
        s_n = jnp.where(kq <= thr_n, s_w[:, 2 * ATTN_BLOCK:], NEG)
        s_c = jnp.dot(q, ckt_ref[...], preferred_element_type=F32)
        return s_p, s_0, s_n, s_c

    def softmax(s_p, s_0, s_n, s_c):
        m = jnp.maximum(jnp.maximum(s_p, s_0), jnp.maximum(s_n, _fold_lanes(s_c, jnp.maximum)))
        m = jnp.maximum(jnp.max(m, axis=-1, keepdims=True), sk)
        p_p = jnp.exp2(s_p - m)
        p_0 = jnp.exp2(s_0 - m)
        p_n = jnp.exp2(s_n - m)
        p_c = jnp.exp2(s_c - m)
        denom = (jnp.sum((p_p + p_0) + (p_n + _fold_lanes(p_c, jnp.add)), axis=-1, keepdims=True)
                 + jnp.exp2(sk - m))
        p_w = jnp.concatenate([p_p.astype(BF16), p_0.astype(BF16), p_n.astype(BF16)], axis=1)
        return p_w, p_c.astype(BF16), denom

    def weighted_values(n, p_w, p_c, denom):
        r0 = pl.multiple_of(n * ATTN_BLOCK, ATTN_BLOCK)
        o = (jnp.dot(p_w, vb_ref[pl.ds(r0, 3 * ATTN_BLOCK), :], preferred_element_type=F32)
             + jnp.dot(p_c, cvb_ref[...], preferred_element_type=F32))
        o = o / denom
        for g in range(KV_GROUP):
            o_ref[pl.ds(r0, ATTN_BLOCK), g * HEAD_DIM:(g + 1) * HEAD_DIM] = (
                o[g * ATTN_BLOCK:(g + 1) * ATTN_BLOCK, :].astype(o_ref.dtype))

    def group(i, carry):
        blocks = [i * ATTN_GROUP + j for j in range(ATTN_GROUP)]
        s = [scores(n) for n in blocks]
        p = [softmax(*sn) for sn in s]
        for n, pn in zip(blocks, p):
            weighted_values(n, *pn)
        return carry

    lax.fori_loop(0, n_blocks // ATTN_GROUP, group, 0)


def _rope_tables():
    pos = jnp.arange(LAT_LEN, dtype=jnp.int32)
    row = (pos // GRID_W).astype(F32)
    col = (pos % GRID_W).astype(F32)
    half = HEAD_DIM // 2
    inv_freq = ROPE_BASE ** (-jnp.arange(0, half, 2, dtype=F32) / half)
    ang_r = row[:, None] * inv_freq[None, :]
    ang_c = col[:, None] * inv_freq[None, :]
    zero = jnp.zeros_like(ang_r)
    cos = jnp.concatenate([jnp.cos(ang_r), jnp.cos(ang_r), jnp.cos(ang_c), jnp.cos(ang_c)], axis=1)
    sin_lo = jnp.concatenate([-jnp.sin(ang_r), zero, -jnp.sin(ang_c), zero], axis=1)
    sin_hi = jnp.concatenate([zero, jnp.sin(ang_r), zero, jnp.sin(ang_c)], axis=1)
    return cos, sin_lo, sin_hi


def _attention(qkv, cache_k, cache_v, sink, layer_a):
    q_cols = KV_GROUP * HEAD_DIM
    k_blk0 = N_HEADS
    v_blk0 = N_HEADS + N_KV_HEADS
    ctx_rows0 = T_CTX // LAT_LEN
    cos, sin_lo, sin_hi = _rope_tables()
    q_tabs = [t * Q_SCALE for t in (cos, sin_lo, sin_hi)]
    ck = cache_k.reshape(N_LAT_REQ, -1, PAST_LEN, N_KV_HEADS * HEAD_DIM)
    cv = cache_v.reshape(N_LAT_REQ, -1, PAST_LEN, N_KV_HEADS * HEAD_DIM)
    tab_spec = pl.BlockSpec((LAT_LEN, HEAD_DIM), lambda b, h: (0, 0))
    smem_spec = pl.BlockSpec(memory_space=pltpu.SMEM)
    lat_block = LAT_LEN * HEAD_DIM * 4
    vmem_lat = (2 * (q_cols // HEAD_DIM + 2 + 1 + 6) * lat_block + 2 * LAT_LEN * q_cols * 2
                + 3 * lat_block + 16 * MIB)
    o_lat = pl.pallas_call(
        _attn_lat_kernel,
        grid=(N_LAT_REQ, N_KV_HEADS),
        in_specs=[
            smem_spec,
            pl.BlockSpec((LAT_LEN, q_cols), lambda b, h: (ctx_rows0 + b, h)),
            pl.BlockSpec((LAT_LEN, HEAD_DIM), lambda b, h: (ctx_rows0 + b, k_blk0 + h)),
            pl.BlockSpec((LAT_LEN, HEAD_DIM), lambda b, h: (ctx_rows0 + b, v_blk0 + h)),
            pl.BlockSpec((None, None, PAST_LEN, HEAD_DIM), lambda b, h: (b, layer_a, 0, h)),
            pl.BlockSpec((None, None, PAST_LEN, HEAD_DIM), lambda b, h: (b, layer_a, 0, h)),
            tab_spec, tab_spec, tab_spec, tab_spec, tab_spec, tab_spec,
        ],
        out_specs=pl.BlockSpec((LAT_LEN, q_cols), lambda b, h: (ctx_rows0 + b, h)),
        out_shape=jax.ShapeDtypeStruct((T_ALL, N_HEADS * HEAD_DIM), BF16),
        scratch_shapes=[pltpu.VMEM((LAT_LEN // ATTN_BLOCK, HEAD_DIM, 3 * ATTN_BLOCK), BF16),
                        pltpu.VMEM((LAT_LEN + 2 * ATTN_BLOCK, HEAD_DIM), BF16),
                        pltpu.VMEM((HEAD_DIM, PAST_LEN), BF16), pltpu.VMEM((PAST_LEN, HEAD_DIM), BF16)],
        compiler_params=_params(vmem_lat, 2),
        name="attn_latent",
    )(sink, qkv, qkv, qkv, ck, cv, cos, sin_lo, sin_hi, *q_tabs)

    vmem_ctx = 2 * (CTX_LEN * q_cols * 4 + 2 * CTX_LEN * HEAD_DIM * 4 + CTX_LEN * q_cols * 2) + 8 * MIB
    return pl.pallas_call(
        _attn_ctx_kernel,
        grid=(N_CTX_REQ, N_KV_HEADS),
        in_specs=[
            smem_spec,
            pl.BlockSpec((CTX_LEN, q_cols), lambda b, h: (b, h)),
            pl.BlockSpec((CTX_LEN, HEAD_DIM), lambda b, h: (b, k_blk0 + h)),
            pl.BlockSpec((CTX_LEN, HEAD_DIM), lambda b, h: (b, v_blk0 + h)),
            pl.BlockSpec(memory_space=pl.ANY),
        ],
        out_specs=pl.BlockSpec((CTX_LEN, q_cols), lambda b, h: (b, h)),
        out_shape=jax.ShapeDtypeStruct((T_ALL, N_HEADS * HEAD_DIM), BF16),
        input_output_aliases={4: 0},
        compiler_params=_params(vmem_ctx, 2),
        name="attn_context",
    )(sink, qkv, qkv, qkv, o_lat)


def _dft_tables(n):
    k = jnp.arange(n, dtype=jnp.int32)
    ang = ((k[:, None] * k[None, :]) % n).astype(F32) * (2.0 * math.pi / n)
    return jnp.cos(ang), jnp.sin(ang)


def _fourier_group(x, cs_ref, pos_ref, norm):
    z = jnp.dot(x, cs_ref[...], preferred_element_type=F32)
    zz = jnp.concatenate([z[:, :GROUP_DIM], z[:, GROUP_DIM:]], axis=0).astype(BF16)
    return jnp.dot(pos_ref[...], zz, preferred_element_type=F32) * norm


def _fourier_lat_kernel(x_ref, cs_ref, pos_ref, o_ref):
    norm = (LAT_LEN * GROUP_DIM) ** -0.5
    o_ref[...] = _fourier_group(x_ref[...], cs_ref, pos_ref, norm).astype(o_ref.dtype)


def _fourier_ctx_kernel(x_ref, cs_ref, pos_ref, prev_ref, o_ref):
    del prev_ref
    norm = (CTX_LEN * GROUP_DIM) ** -0.5
    for g in range(N_GROUPS):
        cols = slice(g * GROUP_DIM, (g + 1) * GROUP_DIM)
        o_ref[:, cols] = _fourier_group(x_ref[:, cols], cs_ref, pos_ref, norm).astype(o_ref.dtype)


def _fourier(h):
    cc, sc = _dft_tables(GROUP_DIM)
    cs = jnp.concatenate([cc, sc], axis=1).astype(BF16)
    cl, sl = _dft_tables(LAT_LEN)
    pos_lat = jnp.concatenate([cl, -sl], axis=1).astype(BF16)
    cp, sp = _dft_tables(CTX_LEN)
    pos_ctx = jnp.concatenate([cp, -sp], axis=1).astype(BF16)
    ctx_rows0 = T_CTX // LAT_LEN

    blk = LAT_LEN * GROUP_DIM
    vmem_lat = 2 * (2 * blk * 2 + GROUP_DIM * 2 * GROUP_DIM * 2 + LAT_LEN * 2 * LAT_LEN * 2) + 5 * blk * 4 + 4 * MIB
    f_lat = pl.pallas_call(
        _fourier_lat_kernel,
        grid=(N_LAT_REQ, N_GROUPS),
        in_specs=[pl.BlockSpec((LAT_LEN, GROUP_DIM), lambda b, g: (ctx_rows0 + b, g)),
                  pl.BlockSpec((GROUP_DIM, 2 * GROUP_DIM), lambda b, g: (0, 0)),
                  pl.BlockSpec((LAT_LEN, 2 * LAT_LEN), lambda b, g: (0, 0))],
        out_specs=pl.BlockSpec((LAT_LEN, GROUP_DIM), lambda b, g: (ctx_rows0 + b, g)),
        out_shape=jax.ShapeDtypeStruct((T_ALL, D_MODEL), BF16),
        compiler_params=_params(vmem_lat, 2),
        name="fourier_latent",
    )(h, cs, pos_lat)

    vmem_ctx = 2 * (2 * CTX_LEN * D_MODEL * 2 + GROUP_DIM * 2 * GROUP_DIM * 2 + CTX_LEN * 2 * CTX_LEN * 2) + 16 * MIB
    return pl.pallas_call(
        _fourier_ctx_kernel,
        grid=(N_CTX_REQ,),
        in_specs=[pl.BlockSpec((CTX_LEN, D_MODEL), lambda b: (b, 0)),
                  pl.BlockSpec((GROUP_DIM, 2 * GROUP_DIM), lambda b: (0, 0)),
                  pl.BlockSpec((CTX_LEN, 2 * CTX_LEN), lambda b: (0, 0)),
                  pl.BlockSpec(memory_space=pl.ANY)],
        out_specs=pl.BlockSpec((CTX_LEN, D_MODEL), lambda b: (b, 0)),
        out_shape=jax.ShapeDtypeStruct((T_ALL, D_MODEL), BF16),
        input_output_aliases={3: 0},
        compiler_params=_params(vmem_ctx, 1),
        name="fourier_context",
    )(h, cs, pos_ctx, f_lat)


def kernel(x_prompt, x_sample, cache_k, cache_v, c, c_ctx, w_ada, b_ada, g_mix_pre, g_mix_post,
           g_ffn_pre, g_ffn_post, w_qkv, w_attn_out, attn_sink, w_fourier, w_ffn_in, w_ffn_out):
    xp = x_prompt.reshape(T_CTX, D_MODEL)
    xs = x_sample.reshape(T_LAT, D_MODEL)
    cond = jnp.concatenate(
        [c, c_ctx[None, :], jnp.zeros((MOD_ROWS - N_LAT_REQ - 1, D_MODEL), F32)], axis=0)
    mod = _ada_table(cond, w_ada, b_ada).reshape(DEPTH, MOD_ROWS, 6, D_MODEL)
    g_mix_pre, g_mix_post, g_ffn_pre, g_ffn_post = (
        g.reshape(DEPTH, 1, D_MODEL) for g in (g_mix_pre, g_mix_post, g_ffn_pre, g_ffn_post))
    new_k, new_v = [], []
    y = (xp, xs)
    h = _pre_call(xp, xs, g_mix_pre, mod, 0, 0)
    for i in range(DEPTH):
        if i % 2 == 0:
            a = i // 2
            qkv = _project(h, w_qkv, a, rows=MM_ROWS, cols=1024, chunk_rows=256, out_dtype=F32,
                           name="qkv_proj")
            k0 = N_HEADS * HEAD_DIM
            k1 = k0 + N_KV_HEADS * HEAD_DIM
            new_k.append(qkv[:T_CTX, k0:k1].reshape(N_CTX_REQ, CTX_LEN, N_KV_HEADS, HEAD_DIM))
            new_v.append(qkv[:T_CTX, k1:].reshape(N_CTX_REQ, CTX_LEN, N_KV_HEADS, HEAD_DIM))
            att = _attention(qkv, cache_k, cache_v, attn_sink[a], a)
            o = _project(att, w_attn_out, a, rows=MM_ROWS, cols=1024, chunk_rows=256, out_dtype=BF16,
                         name="attn_out_proj")
        else:
            o = _project(_fourier(h), w_fourier, i // 2, rows=MM_ROWS, cols=1024, chunk_rows=256,
                         out_dtype=BF16, name="fourier_proj")
        y, h = _postpre_call(o, y, g_mix_post, mod, i, 0, g_ffn_pre, i, 1)
        act = _swiglu_in(h, w_ffn_in, i)
        o = _project(act, w_ffn_out, i, rows=FFN_OUT_ROWS, cols=512, chunk_rows=256, out_dtype=BF16,
                     name="ffn_out_proj")
        if i + 1 < DEPTH:
            y, h = _postpre_call(o, y, g_ffn_post, mod, i, 1, g_mix_pre, i + 1, 0)
        else:
            y_p, y_s = _post_call(o, y, g_ffn_post, mod, i, 1)

    return (y_p.reshape(N_CTX_REQ, CTX_LEN, D_MODEL), y_s.reshape(N_LAT_REQ, LAT_LEN, D_MODEL),
            jnp.stack(new_k, axis=1), jnp.stack(new_v, axis=1))
```

```python
import functools
import math

import jax
import jax.numpy as jnp
from jax import lax
from jax.experimental import pallas as pl
from jax.experimental.pallas import tpu as pltpu

F32 = jnp.float32
BF16 = jnp.bfloat16

D_MODEL = 4096
N_CTX_REQ = 16
CTX_LEN = 256
N_LAT_REQ = 8
LAT_LEN = 1024
PAST_LEN = 512
DEPTH = 4
GRID_W = 64
HEAD_DIM = 128
N_HEADS = 32
N_KV_HEADS = 8
KV_GROUP = 4
ATTN_BLOCK = 128
CTX_KV_PER_STEP = 4
ATTN_GROUP = 8
ROPE_BASE = 10000.0
N_GROUPS = 8
GROUP_DIM = D_MODEL // N_GROUPS
D_FF = 11008
EPS = 1e-6
NEG = -1e30

T_CTX = N_CTX_REQ * CTX_LEN
T_LAT = N_LAT_REQ * LAT_LEN
T_ALL = T_CTX + T_LAT
MOD_ROWS = 16
CTX_MOD_ROW = N_LAT_REQ
QKV_W = (N_HEADS + 2 * N_KV_HEADS) * HEAD_DIM
ATTN_SCALE = HEAD_DIM ** -0.5
LOG2E = math.log2(math.e)
Q_SCALE = ATTN_SCALE * LOG2E

V7X_VMEM_BYTES = 64 * 1024 * 1024
MIB = 1024 * 1024

ELEM_ROWS = 256
MM_ROWS = 1024
FFN_IN_ROWS = 2048
FFN_OUT_ROWS = 512
FFN_TILE = 256


def _params(vmem_bytes, ngrid):
    assert vmem_bytes <= V7X_VMEM_BYTES
    return pltpu.CompilerParams(
        dimension_semantics=("arbitrary",) * ngrid,
        vmem_limit_bytes=int(vmem_bytes),
    )


def _mod_row_of_block(i, rows_per_block):
    ctx_blocks = T_CTX // rows_per_block
    blocks_per_lat = LAT_LEN // rows_per_block
    return jnp.where(i < ctx_blocks, CTX_MOD_ROW, (i - ctx_blocks) // blocks_per_lat)


ADA_TILE = 512


def _ada_kernel(c_ref, w_ref, b_ref, o_ref):
    c = c_ref[...]
    s = (c * jax.nn.sigmoid(c)).astype(BF16)
    w = w_ref[0].astype(BF16)
    o_ref[0] = jnp.dot(s, w, preferred_element_type=F32) + b_ref[0]


def _ada_table(cond, w_ada, b_ada):
    n_out = 6 * D_MODEL
    vmem = 2 * (D_MODEL * ADA_TILE * 4) + D_MODEL * ADA_TILE * 2 + 4 * MIB
    return pl.pallas_call(
        _ada_kernel,
        grid=(DEPTH, n_out // ADA_TILE),
        in_specs=[
            pl.BlockSpec((MOD_ROWS, D_MODEL), lambda l, j: (0, 0)),
            pl.BlockSpec((1, D_MODEL, ADA_TILE), lambda l, j: (l, 0, j)),
            pl.BlockSpec((1, 1, ADA_TILE), lambda l, j: (l, 0, j)),
        ],
        out_specs=pl.BlockSpec((1, MOD_ROWS, ADA_TILE), lambda l, j: (l, 0, j)),
        out_shape=jax.ShapeDtypeStruct((DEPTH, MOD_ROWS, n_out), F32),
        compiler_params=_params(vmem, 2),
        name="ada_table",
    )(cond, w_ada, b_ada.reshape(DEPTH, 1, n_out))


def _normed(x):
    return x * lax.rsqrt(jnp.mean(x * x, axis=-1, keepdims=True) + EPS)


_CTX_BLOCKS = T_CTX // ELEM_ROWS
ELEM_CHUNK = 16


def _for_row_chunks(body):
    def step(c, carry):
        body(pl.ds(pl.multiple_of(c * ELEM_CHUNK, ELEM_CHUNK), ELEM_CHUNK))
        return carry
    lax.fori_loop(0, ELEM_ROWS // ELEM_CHUNK, step, 0, unroll=4)


def _rows(v):
    return jnp.broadcast_to(v, (ELEM_CHUNK, D_MODEL))


def _set_post_vec(rv_ref, k, gpost_ref, mod_ref, sub):
    rv_ref[k] = _rows(mod_ref[pl.ds(3 * sub + 2, 1), :] * gpost_ref[...])


def _set_pre_vecs(rv_ref, k, gpre_ref, mod_ref, sub):
    rv_ref[k] = _rows(gpre_ref[...] * (1.0 + mod_ref[pl.ds(3 * sub + 1, 1), :]))
    rv_ref[k + 1] = _rows(mod_ref[pl.ds(3 * sub, 1), :])


def _read_stream(yp_ref, ys_ref, rows):
    return jnp.where(pl.program_id(0) < _CTX_BLOCKS, yp_ref[rows, :], ys_ref[rows, :])


def _pre_kernel(sub, yp_ref, ys_ref, gpre_ref, mod_ref, h_ref, rv_ref):
    _set_pre_vecs(rv_ref, 0, gpre_ref, mod_ref, sub)

    def body(rows):
        h_ref[rows, :] = (_normed(_read_stream(yp_ref, ys_ref, rows)) * rv_ref[0] + rv_ref[1]).astype(h_ref.dtype)
    _for_row_chunks(body)


def _post_kernel(sub, o_ref, y_ref, gpost_ref, mod_ref, yp_ref, ys_ref, rv_ref):
    _set_post_vec(rv_ref, 0, gpost_ref, mod_ref, sub)

    def residual(rows):
        return y_ref[rows, :] + rv_ref[0] * _normed(o_ref[rows, :].astype(F32))

    @pl.when(pl.program_id(0) < _CTX_BLOCKS)
    def _():
        def body(rows):
            yp_ref[rows, :] = residual(rows)
        _for_row_chunks(body)

    @pl.when(pl.program_id(0) >= _CTX_BLOCKS)
    def _():
        def body(rows):
            ys_ref[rows, :] = residual(rows)
        _for_row_chunks(body)


def _postpre_kernel(sub, nsub, split_in, o_ref, *refs):
    if split_in:
        yp_ref, ys_ref, gpost_ref, mod_ref, gpre_ref, nmod_ref, yo_ref, h_ref, rv_ref = refs
    else:
        y_ref, gpost_ref, mod_ref, gpre_ref, nmod_ref, yo_ref, h_ref, rv_ref = refs
    _set_post_vec(rv_ref, 0, gpost_ref, mod_ref, sub)
    _set_pre_vecs(rv_ref, 1, gpre_ref, nmod_ref, nsub)

    def body(rows):
        y = _read_stream(yp_ref, ys_ref, rows) if split_in else y_ref[rows, :]
        yn = y + rv_ref[0] * _normed(o_ref[rows, :].astype(F32))
        yo_ref[rows, :] = yn
        h_ref[rows, :] = (_normed(yn) * rv_ref[1] + rv_ref[2]).astype(h_ref.dtype)
    _for_row_chunks(body)


def _rv_scratch(n):
    return [pltpu.VMEM((n, ELEM_CHUNK, D_MODEL), F32)]


def _row_spec():
    return pl.BlockSpec((ELEM_ROWS, D_MODEL), lambda i: (i, 0))


def _split_specs():
    return [pl.BlockSpec((ELEM_ROWS, D_MODEL), lambda i: (jnp.minimum(i, _CTX_BLOCKS - 1), 0)),
            pl.BlockSpec((ELEM_ROWS, D_MODEL), lambda i: (jnp.maximum(i - _CTX_BLOCKS, 0), 0))]


def _gain_spec(layer):
    return pl.BlockSpec((None, 1, D_MODEL), lambda i: (layer, 0, 0))


def _mod_spec(layer):
    return pl.BlockSpec((None, None, 6, D_MODEL),
                        lambda i: (layer, _mod_row_of_block(i, ELEM_ROWS), 0, 0))


_ELEM_BLOCK_F32 = ELEM_ROWS * D_MODEL * 4


def _pre_call(xp, xs, g_pre, mod, layer, sub):
    vmem = 2 * (2 * _ELEM_BLOCK_F32 + _ELEM_BLOCK_F32 // 2) + 3 * _ELEM_BLOCK_F32 + 2 * MIB
    return pl.pallas_call(
        functools.partial(_pre_kernel, sub),
        grid=(T_ALL // ELEM_ROWS,),
        in_specs=_split_specs() + [_gain_spec(layer), _mod_spec(layer)],
        out_specs=_row_spec(),
        out_shape=jax.ShapeDtypeStruct((T_ALL, D_MODEL), BF16),
        scratch_shapes=_rv_scratch(2),
        compiler_params=_params(vmem, 1),
        name="pre_norm",
    )(xp, xs, g_pre, mod)


def _post_call(o, y, g_post, mod, layer, sub):
    vmem = 2 * 4 * _ELEM_BLOCK_F32 + 3 * _ELEM_BLOCK_F32 + 2 * MIB
    return pl.pallas_call(
        functools.partial(_post_kernel, sub),
        grid=(T_ALL // ELEM_ROWS,),
        in_specs=[_row_spec(), _row_spec(), _gain_spec(layer), _mod_spec(layer)],
        out_specs=_split_specs(),
        out_shape=[jax.ShapeDtypeStruct((T_CTX, D_MODEL), F32),
                   jax.ShapeDtypeStruct((T_LAT, D_MODEL), F32)],
        scratch_shapes=_rv_scratch(1),
        compiler_params=_params(vmem, 1),
        name="post_norm",
    )(o, y, g_post, mod)


def _postpre_call(o, y, g_post, mod, layer, sub, g_pre, nlayer, nsub):
    split_in = isinstance(y, tuple)
    y_args = y if split_in else (y,)
    y_specs = _split_specs() if split_in else [_row_spec()]
    vmem = 2 * ((2 + len(y_args)) * _ELEM_BLOCK_F32 + _ELEM_BLOCK_F32 // 2) + 4 * _ELEM_BLOCK_F32 + 2 * MIB
    return pl.pallas_call(
        functools.partial(_postpre_kernel, sub, nsub, split_in),
        grid=(T_ALL // ELEM_ROWS,),
        in_specs=[_row_spec()] + y_specs + [_gain_spec(layer), _mod_spec(layer),
                                            _gain_spec(nlayer), _mod_spec(nlayer)],
        out_specs=[_row_spec(), _row_spec()],
        out_shape=[jax.ShapeDtypeStruct((T_ALL, D_MODEL), F32),
                   jax.ShapeDtypeStruct((T_ALL, D_MODEL), BF16)],
        scratch_shapes=_rv_scratch(3),
        compiler_params=_params(vmem, 1),
        name="post_pre_norm",
    )(o, *y_args, g_post, mod, g_pre, mod)


STREAM_SLOTS = 2


def _aligned(v, m):
    return v if isinstance(v, int) else pl.multiple_of(v, m)


def _wstream_kernel(layer, seg_starts, seg_width, chunk_rows, n_chunks, epilogue,
                    x_ref, w_hbm, o_ref, wb0_ref, wb1_ref, stage_ref, sem):
    j = pl.program_id(0)
    i = pl.program_id(1)
    nj = pl.num_programs(0)
    ni = pl.num_programs(1)
    step = j * ni + i
    set_new = lax.rem(step, 2)
    set_old = 1 - set_new

    def chunk_copies(tile, c, set_, slot):
        r0 = _aligned(c * chunk_rows, chunk_rows)
        return [pltpu.make_async_copy(
            w_hbm.at[layer, pl.ds(r0, chunk_rows), pl.ds(_aligned(start(tile), seg_width), seg_width)],
            stage_ref.at[set_, slot, :, pl.ds(s * seg_width, seg_width)],
            sem.at[set_, slot]) for s, start in enumerate(seg_starts)]

    def group_chunk(group, slot):
        return jnp.minimum(group * STREAM_SLOTS + slot, n_chunks - 1)

    def start_group(tile, group, set_):
        for slot in range(STREAM_SLOTS):
            for cp in chunk_copies(tile, group_chunk(group, slot), set_, slot):
                cp.start()

    def round_chunk(wb_ref, c, set_, slot):
        wb_ref[pl.ds(_aligned(c * chunk_rows, chunk_rows), chunk_rows), :] = stage_ref[set_, slot].astype(BF16)

    @pl.when(step == 0)
    def _():
        for cp in chunk_copies(0, 0, 0, 0):
            cp.start()
        for c in range(n_chunks):
            if c + 1 < n_chunks:
                for cp in chunk_copies(0, c + 1, (c + 1) % 2, 0):
                    cp.start()
            for cp in chunk_copies(0, c, c % 2, 0):
                cp.wait()
            round_chunk(wb0_ref, c, c % 2, 0)
        start_group(jnp.minimum(1, nj - 1), 0, 1)

    for slot in range(STREAM_SLOTS):
        for cp in chunk_copies(0, 0, set_old, slot):
            cp.wait()

    @pl.when(step < nj * ni - 1)
    def _():
        wrap = i == ni - 1
        tile = jnp.minimum(jnp.where(wrap, j + 2, j + 1), nj - 1)
        start_group(tile, jnp.where(wrap, 0, i + 1), set_new)

    for parity, (wb_cur, wb_idle) in enumerate(((wb0_ref, wb1_ref), (wb1_ref, wb0_ref))):
        @pl.when(lax.rem(j, 2) == parity)
        def _(wb_cur=wb_cur, wb_idle=wb_idle):
            for slot in range(STREAM_SLOTS):
                round_chunk(wb_idle, group_chunk(i, slot), set_old, slot)
            acc = jnp.dot(x_ref[...], wb_cur[...], preferred_element_type=F32)
            o_ref[...] = epilogue(acc).astype(o_ref.dtype)


def _matmul_wstream(x, w_stack, layer, *, rows, seg_starts, seg_width, out_cols, chunk_rows, out_dtype,
                    epilogue, name):
    m, k = x.shape
    cols = len(seg_starts) * seg_width
    n_tiles = w_stack.shape[2] // cols
    n_row_blocks = m // rows
    n_chunks = k // chunk_rows
    assert m % rows == 0 and k % chunk_rows == 0 and w_stack.shape[2] % cols == 0
    assert n_row_blocks * STREAM_SLOTS >= n_chunks >= 2
    out_bytes = jnp.dtype(out_dtype).itemsize
    vmem = (2 * rows * k * 2 + 2 * k * cols * 2 + 2 * STREAM_SLOTS * chunk_rows * cols * 4
            + 2 * rows * out_cols * out_bytes + 2 * rows * cols * 4 + 2 * MIB)
    return pl.pallas_call(
        functools.partial(_wstream_kernel, layer, seg_starts, seg_width, chunk_rows, n_chunks, epilogue),
        grid=(n_tiles, n_row_blocks),
        in_specs=[pl.BlockSpec((rows, k), lambda j, i: (i, 0)),
                  pl.BlockSpec(memory_space=pl.ANY)],
        out_specs=pl.BlockSpec((rows, out_cols), lambda j, i: (i, j)),
        out_shape=jax.ShapeDtypeStruct((m, n_tiles * out_cols), out_dtype),
        scratch_shapes=[pltpu.VMEM((k, cols), BF16), pltpu.VMEM((k, cols), BF16),
                        pltpu.VMEM((2, STREAM_SLOTS, chunk_rows, cols), F32),
                        pltpu.SemaphoreType.DMA((2, STREAM_SLOTS))],
        compiler_params=_params(vmem, 2),
        name=name,
    )(x, w_stack)


def _identity(acc):
    return acc


def _project(x, w_stack, layer, *, rows, cols, chunk_rows, out_dtype, name):
    return _matmul_wstream(x, w_stack, layer, rows=rows, seg_starts=[lambda t: t * cols], seg_width=cols,
                           out_cols=cols, chunk_rows=chunk_rows, out_dtype=out_dtype, epilogue=_identity,
                           name=name)


def _swiglu(acc):
    g = acc[:, :FFN_TILE]
    u = acc[:, FFN_TILE:]
    return g * jax.nn.sigmoid(g) * u


def _swiglu_in(h, w_in_stack, layer):
    return _matmul_wstream(
        h, w_in_stack, layer, rows=FFN_IN_ROWS,
        seg_starts=[lambda t: t * FFN_TILE, lambda t: D_FF + t * FFN_TILE], seg_width=FFN_TILE,
        out_cols=FFN_TILE, chunk_rows=512, out_dtype=BF16, epilogue=_swiglu, name="ffn_in_swiglu")


def _attn_ctx_kernel(sink_ref, q_ref, k_ref, v_ref, prev_ref, o_ref):
    del prev_ref
    head0 = pl.program_id(1) * (CTX_KV_PER_STEP * KV_GROUP)
    heads = []
    for kl in range(CTX_KV_PER_STEP):
        kv_cols = slice(kl * HEAD_DIM, (kl + 1) * HEAD_DIM)
        kt = k_ref[:, kv_cols].T.astype(BF16)
        v = v_ref[:, kv_cols].astype(BF16)
        for g in range(KV_GROUP):
            hl = kl * KV_GROUP + g
            heads.append((slice(hl * HEAD_DIM, (hl + 1) * HEAD_DIM), kt, v, sink_ref[head0 + hl] * LOG2E))
    scores = [jnp.dot((q_ref[:, cols] * Q_SCALE).astype(BF16), kt, preferred_element_type=F32)
              for cols, kt, _, _ in heads]
    probs = []
    for (_, _, _, sk), s in zip(heads, scores):
        m = jnp.maximum(jnp.max(_fold_lanes(s, jnp.maximum), axis=-1, keepdims=True), sk)
        p = jnp.exp2(s - m)
        denom = jnp.sum(_fold_lanes(p, jnp.add), axis=-1, keepdims=True) + jnp.exp2(sk - m)
        probs.append((p.astype(BF16), denom))
    for (cols, _, v, _), (p, denom) in zip(heads, probs):
        o = jnp.dot(p, v, preferred_element_type=F32)
        o_ref[:, cols] = (o / denom).astype(o_ref.dtype)


def _fold_lanes(x, op):
    tiles = [x[:, t * HEAD_DIM:(t + 1) * HEAD_DIM] for t in range(x.shape[1] // HEAD_DIM)]
    while len(tiles) > 1:
        tiles = [op(tiles[t], tiles[t + 1]) for t in range(0, len(tiles), 2)]
    return tiles[0]


def _rope(x, cos, sin_lo, sin_hi):
    return x * cos + pltpu.roll(x, 96, 1) * sin_lo + pltpu.roll(x, 32, 1) * sin_hi


def _attn_lat_kernel(sink_ref, q_ref, k_ref, v_ref, ck_ref, cv_ref, cos_ref, slo_ref, shi_ref,
                     cosq_ref, sloq_ref, shiq_ref, o_ref, kt_ref, vb_ref, ckt_ref, cvb_ref):
    kh = pl.program_id(1)
    n_blocks = LAT_LEN // ATTN_BLOCK
    kt = []
    for n in range(n_blocks):
        blk = slice(n * ATTN_BLOCK, (n + 1) * ATTN_BLOCK)
        kt.append(_rope(k_ref[blk, :], cos_ref[blk, :], slo_ref[blk, :], shi_ref[blk, :]).T.astype(BF16))
    pad = jnp.zeros((HEAD_DIM, ATTN_BLOCK), BF16)
    kt = [pad] + kt + [pad]
    for n in range(n_blocks):
        kt_ref[n] = jnp.concatenate(kt[n:n + 3], axis=1)
    vb_ref[:ATTN_BLOCK, :] = pad
    vb_ref[pl.ds(ATTN_BLOCK, LAT_LEN), :] = v_ref[...].astype(BF16)
    vb_ref[pl.ds(ATTN_BLOCK + LAT_LEN, ATTN_BLOCK), :] = pad
    ckt_ref[...] = ck_ref[...].T.astype(BF16)
    cvb_ref[...] = cv_ref[...].astype(BF16)

    rows = KV_GROUP * ATTN_BLOCK
    kq = (lax.broadcasted_iota(jnp.int32, (rows, ATTN_BLOCK), 1)
          - lax.broadcasted_iota(jnp.int32, (rows, ATTN_BLOCK), 0) % ATTN_BLOCK)
    sk = jnp.concatenate(
        [jnp.full((ATTN_BLOCK, 1), sink_ref[kh * KV_GROUP + g] * LOG2E, F32) for g in range(KV_GROUP)],
        axis=0)

    def scores(n):
        r0 = pl.multiple_of(n * ATTN_BLOCK, ATTN_BLOCK)
        cos = cosq_ref[pl.ds(r0, ATTN_BLOCK), :]
        slo = sloq_ref[pl.ds(r0, ATTN_BLOCK), :]
        shi = shiq_ref[pl.ds(r0, ATTN_BLOCK), :]
        q = jnp.concatenate(
            [_rope(q_ref[pl.ds(r0, ATTN_BLOCK), g * HEAD_DIM:(g + 1) * HEAD_DIM], cos, slo, shi).astype(BF16)
             for g in range(KV_GROUP)], axis=0)
        thr_p = jnp.where(n >= 1, 0, 2 * ATTN_BLOCK)
        thr_n = jnp.where(n <= n_blocks - 2, 0, -2 * ATTN_BLOCK)
        s_w = jnp.dot(q, kt_ref[n], preferred_element_type=F32)
        s_p = jnp.where(kq >= thr_p, s_w[:, :ATTN_BLOCK], NEG)
        s_0 = s_w[:, ATTN_BLOCK:2 * ATTN_BLOCK]
        s_n = jnp.where(kq <= thr_n, s_w[:, 2 * ATTN_BLOCK:], NEG)
        s_c = jnp.dot(q, ckt_ref[...], preferred_element_type=F32)
        return s_p, s_0, s_n, s_c

    def softmax(s_p, s_0, s_n, s_c):
        m = jnp.maximum(jnp.maximum(s_p, s_0), jnp.maximum(s_n, _fold_lanes(s_c, jnp.maximum)))
        m = jnp.maximum(jnp.max(m, axis=-1, keepdims=True), sk)
        p_p = jnp.exp2(s_p - m)
        p_0 = jnp.exp2(s_0 - m)
        p_n = jnp.exp2(s_n - m)
        p_c = jnp.exp2(s_c - m)
        denom = (jnp.sum((p_p + p_0) + (p_n + _fold_lanes(p_c, jnp.add)), axis=-1, keepdims=True)
                 + jnp.exp2(sk - m))
        p_w = jnp.concatenate([p_p.astype(BF16), p_0.astype(BF16), p_n.astype(BF16)], axis=1)
        return p_w, p_c.astype(BF16), denom

    def weighted_values(n, p_w, p_c, denom):
        r0 = pl.multiple_of(n * ATTN_BLOCK, ATTN_BLOCK)
        o = (jnp.dot(p_w, vb_ref[pl.ds(r0, 3 * ATTN_BLOCK), :], preferred_element_type=F32)
             + jnp.dot(p_c, cvb_ref[...], preferred_element_type=F32))
        o = o / denom
        for g in range(KV_GROUP):
            o_ref[pl.ds(r0, ATTN_BLOCK), g * HEAD_DIM:(g + 1) * HEAD_DIM] = (
                o[g * ATTN_BLOCK:(g + 1) * ATTN_BLOCK, :].astype(o_ref.dtype))

    def group(i, carry):
        blocks = [i * ATTN_GROUP + j for j in range(ATTN_GROUP)]
        s = [scores(n) for n in blocks]
        p = [softmax(*sn) for sn in s]
        for n, pn in zip(blocks, p):
            weighted_values(n, *pn)
        return carry

    lax.fori_loop(0, n_blocks // ATTN_GROUP, group, 0)


def _rope_tables():
    pos = jnp.arange(LAT_LEN, dtype=jnp.int32)
    row = (pos // GRID_W).astype(F32)
    col = (pos % GRID_W).astype(F32)
    half = HEAD_DIM // 2
    inv_freq = ROPE_BASE ** (-jnp.arange(0, half, 2, dtype=F32) / half)
    ang_r = row[:, None] * inv_freq[None, :]
    ang_c = col[:, None] * inv_freq[None, :]
    zero = jnp.zeros_like(ang_r)
    cos = jnp.concatenate([jnp.cos(ang_r), jnp.cos(ang_r), jnp.cos(ang_c), jnp.cos(ang_c)], axis=1)
    sin_lo = jnp.concatenate([-jnp.sin(ang_r), zero, -jnp.sin(ang_c), zero], axis=1)
    sin_hi = jnp.concatenate([zero, jnp.sin(ang_r), zero, jnp.sin(ang_c)], axis=1)
    return cos, sin_lo, sin_hi


def _attention(qkv, cache_k, cache_v, sink, layer_a):
    q_cols = KV_GROUP * HEAD_DIM
    k_blk0 = N_HEADS
    v_blk0 = N_HEADS + N_KV_HEADS
    ctx_rows0 = T_CTX // LAT_LEN
    cos, sin_lo, sin_hi = _rope_tables()
    q_tabs = [t * Q_SCALE for t in (cos, sin_lo, sin_hi)]
    ck = cache_k.reshape(N_LAT_REQ, -1, PAST_LEN, N_KV_HEADS * HEAD_DIM)
    cv = cache_v.reshape(N_LAT_REQ, -1, PAST_LEN, N_KV_HEADS * HEAD_DIM)
    tab_spec = pl.BlockSpec((LAT_LEN, HEAD_DIM), lambda b, h: (0, 0))
    smem_spec = pl.BlockSpec(memory_space=pltpu.SMEM)
    lat_block = LAT_LEN * HEAD_DIM * 4
    vmem_lat = (2 * (q_cols // HEAD_DIM + 2 + 1 + 6) * lat_block + 2 * LAT_LEN * q_cols * 2
                + 3 * lat_block + 16 * MIB)
    o_lat = pl.pallas_call(
        _attn_lat_kernel,
        grid=(N_LAT_REQ, N_KV_HEADS),
        in_specs=[
            smem_spec,
            pl.BlockSpec((LAT_LEN, q_cols), lambda b, h: (ctx_rows0 + b, h)),
            pl.BlockSpec((LAT_LEN, HEAD_DIM), lambda b, h: (ctx_rows0 + b, k_blk0 + h)),
            pl.BlockSpec((LAT_LEN, HEAD_DIM), lambda b, h: (ctx_rows0 + b, v_blk0 + h)),
            pl.BlockSpec((None, None, PAST_LEN, HEAD_DIM), lambda b, h: (b, layer_a, 0, h)),
            pl.BlockSpec((None, None, PAST_LEN, HEAD_DIM), lambda b, h: (b, layer_a, 0, h)),
            tab_spec, tab_spec, tab_spec, tab_spec, tab_spec, tab_spec,
        ],
        out_specs=pl.BlockSpec((LAT_LEN, q_cols), lambda b, h: (ctx_rows0 + b, h)),
        out_shape=jax.ShapeDtypeStruct((T_ALL, N_HEADS * HEAD_DIM), BF16),
        scratch_shapes=[pltpu.VMEM((LAT_LEN // ATTN_BLOCK, HEAD_DIM, 3 * ATTN_BLOCK), BF16),
                        pltpu.VMEM((LAT_LEN + 2 * ATTN_BLOCK, HEAD_DIM), BF16),
                        pltpu.VMEM((HEAD_DIM, PAST_LEN), BF16), pltpu.VMEM((PAST_LEN, HEAD_DIM), BF16)],
        compiler_params=_params(vmem_lat, 2),
        name="attn_latent",
    )(sink, qkv, qkv, qkv, ck, cv, cos, sin_lo, sin_hi, *q_tabs)

    cq = CTX_KV_PER_STEP * q_cols
    ckv = CTX_KV_PER_STEP * HEAD_DIM
    vmem_ctx = 2 * (CTX_LEN * cq * 4 + 2 * CTX_LEN * ckv * 4 + CTX_LEN * cq * 2) + 16 * MIB
    return pl.pallas_call(
        _attn_ctx_kernel,
        grid=(N_CTX_REQ, N_KV_HEADS // CTX_KV_PER_STEP),
        in_specs=[
            smem_spec,
            pl.BlockSpec((CTX_LEN, cq), lambda b, h: (b, h)),
            pl.BlockSpec((CTX_LEN, ckv), lambda b, h: (b, k_blk0 // CTX_KV_PER_STEP + h)),
            pl.BlockSpec((CTX_LEN, ckv), lambda b, h: (b, v_blk0 // CTX_KV_PER_STEP + h)),
            pl.BlockSpec(memory_space=pl.ANY),
        ],
        out_specs=pl.BlockSpec((CTX_LEN, cq), lambda b, h: (b, h)),
        out_shape=jax.ShapeDtypeStruct((T_ALL, N_HEADS * HEAD_DIM), BF16),
        input_output_aliases={4: 0},
        compiler_params=_params(vmem_ctx, 2),
        name="attn_context",
    )(sink, qkv, qkv, qkv, o_lat)


def _dft_tables(n):
    k = jnp.arange(n, dtype=jnp.int32)
    ang = ((k[:, None] * k[None, :]) % n).astype(F32) * (2.0 * math.pi / n)
    return jnp.cos(ang), jnp.sin(ang)


def _fourier_group(x, cs_ref, pos_ref, norm):
    z = jnp.dot(x, cs_ref[...], preferred_element_type=F32)
    zz = jnp.concatenate([z[:, :GROUP_DIM], z[:, GROUP_DIM:]], axis=0).astype(BF16)
    return jnp.dot(pos_ref[...], zz, preferred_element_type=F32) * norm


def _fourier_lat_kernel(x_ref, cs_ref, pos_ref, o_ref):
    norm = (LAT_LEN * GROUP_DIM) ** -0.5
    o_ref[...] = _fourier_group(x_ref[...], cs_ref, pos_ref, norm).astype(o_ref.dtype)


def _fourier_ctx_kernel(x_ref, cs_ref, pos_ref, prev_ref, o_ref):
    del prev_ref
    norm = (CTX_LEN * GROUP_DIM) ** -0.5
    for g in range(N_GROUPS):
        cols = slice(g * GROUP_DIM, (g + 1) * GROUP_DIM)
        o_ref[:, cols] = _fourier_group(x_ref[:, cols], cs_ref, pos_ref, norm).astype(o_ref.dtype)


def _fourier(h):
    cc, sc = _dft_tables(GROUP_DIM)
    cs = jnp.concatenate([cc, sc], axis=1).astype(BF16)
    cl, sl = _dft_tables(LAT_LEN)
    pos_lat = jnp.concatenate([cl, -sl], axis=1).astype(BF16)
    cp, sp = _dft_tables(CTX_LEN)
    pos_ctx = jnp.concatenate([cp, -sp], axis=1).astype(BF16)
    ctx_rows0 = T_CTX // LAT_LEN

    blk = LAT_LEN * GROUP_DIM
    vmem_lat = 2 * (2 * blk * 2 + GROUP_DIM * 2 * GROUP_DIM * 2 + LAT_LEN * 2 * LAT_LEN * 2) + 5 * blk * 4 + 4 * MIB
    f_lat = pl.pallas_call(
        _fourier_lat_kernel,
        grid=(N_LAT_REQ, N_GROUPS),
        in_specs=[pl.BlockSpec((LAT_LEN, GROUP_DIM), lambda b, g: (ctx_rows0 + b, g)),
                  pl.BlockSpec((GROUP_DIM, 2 * GROUP_DIM), lambda b, g: (0, 0)),
                  pl.BlockSpec((LAT_LEN, 2 * LAT_LEN), lambda b, g: (0, 0))],
        out_specs=pl.BlockSpec((LAT_LEN, GROUP_DIM), lambda b, g: (ctx_rows0 + b, g)),
        out_shape=jax.ShapeDtypeStruct((T_ALL, D_MODEL), BF16),
        compiler_params=_params(vmem_lat, 2),
        name="fourier_latent",
    )(h, cs, pos_lat)

    vmem_ctx = 2 * (2 * CTX_LEN * D_MODEL * 2 + GROUP_DIM * 2 * GROUP_DIM * 2 + CTX_LEN * 2 * CTX_LEN * 2) + 16 * MIB
    return pl.pallas_call(
        _fourier_ctx_kernel,
        grid=(N_CTX_REQ,),
        in_specs=[pl.BlockSpec((CTX_LEN, D_MODEL), lambda b: (b, 0)),
                  pl.BlockSpec((GROUP_DIM, 2 * GROUP_DIM), lambda b: (0, 0)),
                  pl.BlockSpec((CTX_LEN, 2 * CTX_LEN), lambda b: (0, 0)),
                  pl.BlockSpec(memory_space=pl.ANY)],
        out_specs=pl.BlockSpec((CTX_LEN, D_MODEL), lambda b: (b, 0)),
        out_shape=jax.ShapeDtypeStruct((T_ALL, D_MODEL), BF16),
        input_output_aliases={3: 0},
        compiler_params=_params(vmem_ctx, 1),
        name="fourier_context",
    )(h, cs, pos_ctx, f_lat)


def kernel(x_prompt, x_sample, cache_k, cache_v, c, c_ctx, w_ada, b_ada, g_mix_pre, g_mix_post,
           g_ffn_pre, g_ffn_post, w_qkv, w_attn_out, attn_sink, w_fourier, w_ffn_in, w_ffn_out):
    xp = x_prompt.reshape(T_CTX, D_MODEL)
    xs = x_sample.reshape(T_LAT, D_MODEL)
    cond = jnp.concatenate(
        [c, c_ctx[None, :], jnp.zeros((MOD_ROWS - N_LAT_REQ - 1, D_MODEL), F32)], axis=0)
    mod = _ada_table(cond, w_ada, b_ada).reshape(DEPTH, MOD_ROWS, 6, D_MODEL)
    g_mix_pre, g_mix_post, g_ffn_pre, g_ffn_post = (
        g.reshape(DEPTH, 1, D_MODEL) for g in (g_mix_pre, g_mix_post, g_ffn_pre, g_ffn_post))
    new_k, new_v = [], []
    y = (xp, xs)
    h = _pre_call(xp, xs, g_mix_pre, mod, 0, 0)
    for i in range(DEPTH):
        if i % 2 == 0:
            a = i // 2
            qkv = _project(h, w_qkv, a, rows=MM_ROWS, cols=1024, chunk_rows=256, out_dtype=F32,
                           name="qkv_proj")
            k0 = N_HEADS * HEAD_DIM
            k1 = k0 + N_KV_HEADS * HEAD_DIM
            new_k.append(qkv[:T_CTX, k0:k1].reshape(N_CTX_REQ, CTX_LEN, N_KV_HEADS, HEAD_DIM))
            new_v.append(qkv[:T_CTX, k1:].reshape(N_CTX_REQ, CTX_LEN, N_KV_HEADS, HEAD_DIM))
            att = _attention(qkv, cache_k, cache_v, attn_sink[a], a)
            o = _project(att, w_attn_out, a, rows=MM_ROWS, cols=1024, chunk_rows=256, out_dtype=BF16,
                         name="attn_out_proj")
        else:
            o = _project(_fourier(h), w_fourier, i // 2, rows=MM_ROWS, cols=1024, chunk_rows=256,
                         out_dtype=BF16, name="fourier_proj")
        y, h = _postpre_call(o, y, g_mix_post, mod, i, 0, g_ffn_pre, i, 1)
        act = _swiglu_in(h, w_ffn_in, i)
        o = _project(act, w_ffn_out, i, rows=FFN_OUT_ROWS, cols=512, chunk_rows=256, out_dtype=BF16,
                     name="ffn_out_proj")
        if i + 1 < DEPTH:
            y, h = _postpre_call(o, y, g_ffn_post, mod, i, 1, g_mix_pre, i + 1, 0)
        else:
            y_p, y_s = _post_call(o, y, g_ffn_post, mod, i, 1)

    return (y_p.reshape(N_CTX_REQ, CTX_LEN, D_MODEL), y_s.reshape(N_LAT_REQ, LAT_LEN, D_MODEL),
            jnp.stack(new_k, axis=1), jnp.stack(new_v, axis=1))
```

```python
import functools
import math

import jax
import jax.numpy as jnp
from jax import lax
from jax.experimental import pallas as pl
from jax.experimental.pallas import tpu as pltpu

F32 = jnp.float32
BF16 = jnp.bfloat16

D_MODEL = 4096
N_CTX_REQ = 16
CTX_LEN = 256
N_LAT_REQ = 8
LAT_LEN = 1024
PAST_LEN = 512
DEPTH = 4
GRID_W = 64
HEAD_DIM = 128
N_HEADS = 32
N_KV_HEADS = 8
KV_GROUP = 4
ATTN_BLOCK = 128
CTX_KV_PER_STEP = 4
ATTN_GROUP = 8
ROPE_BASE = 10000.0
N_GROUPS = 8
GROUP_DIM = D_MODEL // N_GROUPS
D_FF = 11008
EPS = 1e-6
NEG = -1e30

T_CTX = N_CTX_REQ * CTX_LEN
T_LAT = N_LAT_REQ * LAT_LEN
T_ALL = T_CTX + T_LAT
MOD_ROWS = 16
CTX_MOD_ROW = N_LAT_REQ
QKV_W = (N_HEADS + 2 * N_KV_HEADS) * HEAD_DIM
ATTN_SCALE = HEAD_DIM ** -0.5
LOG2E = math.log2(math.e)
Q_SCALE = ATTN_SCALE * LOG2E

V7X_VMEM_BYTES = 64 * 1024 * 1024
MIB = 1024 * 1024

ELEM_ROWS = 256
MM_ROWS = 1024
FFN_IN_ROWS = 2048
FFN_OUT_ROWS = 512
FFN_TILE = 256


def _params(vmem_bytes, ngrid):
    assert vmem_bytes <= V7X_VMEM_BYTES
    return pltpu.CompilerParams(
        dimension_semantics=("arbitrary",) * ngrid,
        vmem_limit_bytes=int(vmem_bytes),
    )


def _mod_row_of_block(i, rows_per_block):
    ctx_blocks = T_CTX // rows_per_block
    blocks_per_lat = LAT_LEN // rows_per_block
    return jnp.where(i < ctx_blocks, CTX_MOD_ROW, (i - ctx_blocks) // blocks_per_lat)


ADA_TILE = 512


def _ada_kernel(c_ref, w_ref, b_ref, o_ref):
    c = c_ref[...]
    s = (c * jax.nn.sigmoid(c)).astype(BF16)
    w = w_ref[0].astype(BF16)
    o_ref[0] = jnp.dot(s, w, preferred_element_type=F32) + b_ref[0]


def _ada_table(cond, w_ada, b_ada):
    n_out = 6 * D_MODEL
    vmem = 2 * (D_MODEL * ADA_TILE * 4) + D_MODEL * ADA_TILE * 2 + 4 * MIB
    return pl.pallas_call(
        _ada_kernel,
        grid=(DEPTH, n_out // ADA_TILE),
        in_specs=[
            pl.BlockSpec((MOD_ROWS, D_MODEL), lambda l, j: (0, 0)),
            pl.BlockSpec((1, D_MODEL, ADA_TILE), lambda l, j: (l, 0, j)),
            pl.BlockSpec((1, 1, ADA_TILE), lambda l, j: (l, 0, j)),
        ],
        out_specs=pl.BlockSpec((1, MOD_ROWS, ADA_TILE), lambda l, j: (l, 0, j)),
        out_shape=jax.ShapeDtypeStruct((DEPTH, MOD_ROWS, n_out), F32),
        compiler_params=_params(vmem, 2),
        name="ada_table",
    )(cond, w_ada, b_ada.reshape(DEPTH, 1, n_out))


def _normed(x):
    return x * lax.rsqrt(jnp.mean(x * x, axis=-1, keepdims=True) + EPS)


_CTX_BLOCKS = T_CTX // ELEM_ROWS
ELEM_CHUNK = 16


def _for_row_chunks(body):
    def step(c, carry):
        body(pl.ds(pl.multiple_of(c * ELEM_CHUNK, ELEM_CHUNK), ELEM_CHUNK))
        return carry
    lax.fori_loop(0, ELEM_ROWS // ELEM_CHUNK, step, 0, unroll=4)


def _rows(v):
    return jnp.broadcast_to(v, (ELEM_CHUNK, D_MODEL))


def _set_post_vec(rv_ref, k, gpost_ref, mod_ref, sub):
    rv_ref[k] = _rows(mod_ref[pl.ds(3 * sub + 2, 1), :] * gpost_ref[...])


def _set_pre_vecs(rv_ref, k, gpre_ref, mod_ref, sub):
    rv_ref[k] = _rows(gpre_ref[...] * (1.0 + mod_ref[pl.ds(3 * sub + 1, 1), :]))
    rv_ref[k + 1] = _rows(mod_ref[pl.ds(3 * sub, 1), :])


def _read_stream(yp_ref, ys_ref, rows):
    return jnp.where(pl.program_id(0) < _CTX_BLOCKS, yp_ref[rows, :], ys_ref[rows, :])


def _pre_kernel(sub, yp_ref, ys_ref, gpre_ref, mod_ref, h_ref, rv_ref):
    _set_pre_vecs(rv_ref, 0, gpre_ref, mod_ref, sub)

    def body(rows):
        h_ref[rows, :] = (_normed(_read_stream(yp_ref, ys_ref, rows)) * rv_ref[0] + rv_ref[1]).astype(h_ref.dtype)
    _for_row_chunks(body)


def _post_kernel(sub, o_ref, y_ref, gpost_ref, mod_ref, yp_ref, ys_ref, rv_ref):
    _set_post_vec(rv_ref, 0, gpost_ref, mod_ref, sub)

    def residual(rows):
        return y_ref[rows, :] + rv_ref[0] * _normed(o_ref[rows, :].astype(F32))

    @pl.when(pl.program_id(0) < _CTX_BLOCKS)
    def _():
        def body(rows):
            yp_ref[rows, :] = residual(rows)
        _for_row_chunks(body)

    @pl.when(pl.program_id(0) >= _CTX_BLOCKS)
    def _():
        def body(rows):
            ys_ref[rows, :] = residual(rows)
        _for_row_chunks(body)


def _postpre_kernel(sub, nsub, split_in, o_ref, *refs):
    if split_in:
        yp_ref, ys_ref, gpost_ref, mod_ref, gpre_ref, nmod_ref, yo_ref, h_ref, rv_ref = refs
    else:
        y_ref, gpost_ref, mod_ref, gpre_ref, nmod_ref, yo_ref, h_ref, rv_ref = refs
    _set_post_vec(rv_ref, 0, gpost_ref, mod_ref, sub)
    _set_pre_vecs(rv_ref, 1, gpre_ref, nmod_ref, nsub)

    def body(rows):
        y = _read_stream(yp_ref, ys_ref, rows) if split_in else y_ref[rows, :]
        yn = y + rv_ref[0] * _normed(o_ref[rows, :].astype(F32))
        yo_ref[rows, :] = yn
        h_ref[rows, :] = (_normed(yn) * rv_ref[1] + rv_ref[2]).astype(h_ref.dtype)
    _for_row_chunks(body)


def _rv_scratch(n):
    return [pltpu.VMEM((n, ELEM_CHUNK, D_MODEL), F32)]


def _row_spec():
    return pl.BlockSpec((ELEM_ROWS, D_MODEL), lambda i: (i, 0))


def _split_specs():
    return [pl.BlockSpec((ELEM_ROWS, D_MODEL), lambda i: (jnp.minimum(i, _CTX_BLOCKS - 1), 0)),
            pl.BlockSpec((ELEM_ROWS, D_MODEL), lambda i: (jnp.maximum(i - _CTX_BLOCKS, 0), 0))]


def _gain_spec(layer):
    return pl.BlockSpec((None, 1, D_MODEL), lambda i: (layer, 0, 0))


def _mod_spec(layer):
    return pl.BlockSpec((None, None, 6, D_MODEL),
                        lambda i: (layer, _mod_row_of_block(i, ELEM_ROWS), 0, 0))


_ELEM_BLOCK_F32 = ELEM_ROWS * D_MODEL * 4


def _pre_call(xp, xs, g_pre, mod, layer, sub):
    vmem = 2 * (2 * _ELEM_BLOCK_F32 + _ELEM_BLOCK_F32 // 2) + 3 * _ELEM_BLOCK_F32 + 2 * MIB
    return pl.pallas_call(
        functools.partial(_pre_kernel, sub),
        grid=(T_ALL // ELEM_ROWS,),
        in_specs=_split_specs() + [_gain_spec(layer), _mod_spec(layer)],
        out_specs=_row_spec(),
        out_shape=jax.ShapeDtypeStruct((T_ALL, D_MODEL), BF16),
        scratch_shapes=_rv_scratch(2),
        compiler_params=_params(vmem, 1),
        name="pre_norm",
    )(xp, xs, g_pre, mod)


def _post_call(o, y, g_post, mod, layer, sub):
    vmem = 2 * 4 * _ELEM_BLOCK_F32 + 3 * _ELEM_BLOCK_F32 + 2 * MIB
    return pl.pallas_call(
        functools.partial(_post_kernel, sub),
        grid=(T_ALL // ELEM_ROWS,),
        in_specs=[_row_spec(), _row_spec(), _gain_spec(layer), _mod_spec(layer)],
        out_specs=_split_specs(),
        out_shape=[jax.ShapeDtypeStruct((T_CTX, D_MODEL), F32),
                   jax.ShapeDtypeStruct((T_LAT, D_MODEL), F32)],
        scratch_shapes=_rv_scratch(1),
        compiler_params=_params(vmem, 1),
        name="post_norm",
    )(o, y, g_post, mod)


def _postpre_call(o, y, g_post, mod, layer, sub, g_pre, nlayer, nsub):
    split_in = isinstance(y, tuple)
    y_args = y if split_in else (y,)
    y_specs = _split_specs() if split_in else [_row_spec()]
    vmem = 2 * ((2 + len(y_args)) * _ELEM_BLOCK_F32 + _ELEM_BLOCK_F32 // 2) + 4 * _ELEM_BLOCK_F32 + 2 * MIB
    return pl.pallas_call(
        functools.partial(_postpre_kernel, sub, nsub, split_in),
        grid=(T_ALL // ELEM_ROWS,),
        in_specs=[_row_spec()] + y_specs + [_gain_spec(layer), _mod_spec(layer),
                                            _gain_spec(nlayer), _mod_spec(nlayer)],
        out_specs=[_row_spec(), _row_spec()],
        out_shape=[jax.ShapeDtypeStruct((T_ALL, D_MODEL), F32),
                   jax.ShapeDtypeStruct((T_ALL, D_MODEL), BF16)],
        scratch_shapes=_rv_scratch(3),
        compiler_params=_params(vmem, 1),
        name="post_pre_norm",
    )(o, *y_args, g_post, mod, g_pre, mod)


STREAM_SLOTS = 2


def _aligned(v, m):
    return v if isinstance(v, int) else pl.multiple_of(v, m)


def _wstream_kernel(layer, seg_starts, seg_width, chunk_rows, n_chunks, epilogue,
                    x_ref, w_hbm, o_ref, wb0_ref, wb1_ref, stage_ref, sem):
    j = pl.program_id(0)
    i = pl.program_id(1)
    nj = pl.num_programs(0)
    ni = pl.num_programs(1)
    step = j * ni + i
    set_new = lax.rem(step, 2)
    set_old = 1 - set_new

    def chunk_copies(tile, c, set_, slot):
        r0 = _aligned(c * chunk_rows, chunk_rows)
        return [pltpu.make_async_copy(
            w_hbm.at[layer, pl.ds(r0, chunk_rows), pl.ds(_aligned(start(tile), seg_width), seg_width)],
            stage_ref.at[set_, slot, :, pl.ds(s * seg_width, seg_width)],
            sem.at[set_, slot]) for s, start in enumerate(seg_starts)]

    def group_chunk(group, slot):
        return jnp.minimum(group * STREAM_SLOTS + slot, n_chunks - 1)

    def start_group(tile, group, set_):
        for slot in range(STREAM_SLOTS):
            for cp in chunk_copies(tile, group_chunk(group, slot), set_, slot):
                cp.start()

    def round_chunk(wb_ref, c, set_, slot):
        wb_ref[pl.ds(_aligned(c * chunk_rows, chunk_rows), chunk_rows), :] = stage_ref[set_, slot].astype(BF16)

    @pl.when(step == 0)
    def _():
        for cp in chunk_copies(0, 0, 0, 0):
            cp.start()
        for c in range(n_chunks):
            if c + 1 < n_chunks:
                for cp in chunk_copies(0, c + 1, (c + 1) % 2, 0):
                    cp.start()
            for cp in chunk_copies(0, c, c % 2, 0):
                cp.wait()
            round_chunk(wb0_ref, c, c % 2, 0)
        start_group(jnp.minimum(1, nj - 1), 0, 1)

    for slot in range(STREAM_SLOTS):
        for cp in chunk_copies(0, 0, set_old, slot):
            cp.wait()

    @pl.when(step < nj * ni - 1)
    def _():
        wrap = i == ni - 1
        tile = jnp.minimum(jnp.where(wrap, j + 2, j + 1), nj - 1)
        start_group(tile, jnp.where(wrap, 0, i + 1), set_new)

    for parity, (wb_cur, wb_idle) in enumerate(((wb0_ref, wb1_ref), (wb1_ref, wb0_ref))):
        @pl.when(lax.rem(j, 2) == parity)
        def _(wb_cur=wb_cur, wb_idle=wb_idle):
            for slot in range(STREAM_SLOTS):
                round_chunk(wb_idle, group_chunk(i, slot), set_old, slot)
            acc = jnp.dot(x_ref[...], wb_cur[...], preferred_element_type=F32)
            o_ref[...] = epilogue(acc).astype(o_ref.dtype)


def _matmul_wstream(x, w_stack, layer, *, rows, seg_starts, seg_width, out_cols, chunk_rows, out_dtype,
                    epilogue, name):
    m, k = x.shape
    cols = len(seg_starts) * seg_width
    n_tiles = w_stack.shape[2] // cols
    n_row_blocks = m // rows
    n_chunks = k // chunk_rows
    assert m % rows == 0 and k % chunk_rows == 0 and w_stack.shape[2] % cols == 0
    assert n_row_blocks * STREAM_SLOTS >= n_chunks >= 2
    out_bytes = jnp.dtype(out_dtype).itemsize
    vmem = (2 * rows * k * 2 + 2 * k * cols * 2 + 2 * STREAM_SLOTS * chunk_rows * cols * 4
            + 2 * rows * out_cols * out_bytes + 2 * rows * cols * 4 + 2 * MIB)
    return pl.pallas_call(
        functools.partial(_wstream_kernel, layer, seg_starts, seg_width, chunk_rows, n_chunks, epilogue),
        grid=(n_tiles, n_row_blocks),
        in_specs=[pl.BlockSpec((rows, k), lambda j, i: (i, 0)),
                  pl.BlockSpec(memory_space=pl.ANY)],
        out_specs=pl.BlockSpec((rows, out_cols), lambda j, i: (i, j)),
        out_shape=jax.ShapeDtypeStruct((m, n_tiles * out_cols), out_dtype),
        scratch_shapes=[pltpu.VMEM((k, cols), BF16), pltpu.VMEM((k, cols), BF16),
                        pltpu.VMEM((2, STREAM_SLOTS, chunk_rows, cols), F32),
                        pltpu.SemaphoreType.DMA((2, STREAM_SLOTS))],
        compiler_params=_params(vmem, 2),
        name=name,
    )(x, w_stack)


def _identity(acc):
    return acc


def _project(x, w_stack, layer, *, rows, cols, chunk_rows, out_dtype, name):
    return _matmul_wstream(x, w_stack, layer, rows=rows, seg_starts=[lambda t: t * cols], seg_width=cols,
                           out_cols=cols, chunk_rows=chunk_rows, out_dtype=out_dtype, epilogue=_identity,
                           name=name)


def _swiglu(acc):
    g = acc[:, :FFN_TILE]
    u = acc[:, FFN_TILE:]
    return g * jax.nn.sigmoid(g) * u


def _swiglu_in(h, w_in_stack, layer):
    return _matmul_wstream(
        h, w_in_stack, layer, rows=FFN_IN_ROWS,
        seg_starts=[lambda t: t * FFN_TILE, lambda t: D_FF + t * FFN_TILE], seg_width=FFN_TILE,
        out_cols=FFN_TILE, chunk_rows=512, out_dtype=BF16, epilogue=_swiglu, name="ffn_in_swiglu")


def _attn_ctx_kernel(sink_ref, q_ref, k_ref, v_ref, prev_ref, o_ref):
    del prev_ref
    head0 = pl.program_id(1) * (CTX_KV_PER_STEP * KV_GROUP)
    heads = []
    for kl in range(CTX_KV_PER_STEP):
        kv_cols = slice(kl * HEAD_DIM, (kl + 1) * HEAD_DIM)
        kt = k_ref[:, kv_cols].T.astype(BF16)
        v = v_ref[:, kv_cols].astype(BF16)
        for g in range(KV_GROUP):
            hl = kl * KV_GROUP + g
            heads.append((slice(hl * HEAD_DIM, (hl + 1) * HEAD_DIM), kt, v, sink_ref[head0 + hl] * LOG2E))
    scores = [jnp.dot((q_ref[:, cols] * Q_SCALE).astype(BF16), kt, preferred_element_type=F32)
              for cols, kt, _, _ in heads]
    probs = []
    for (_, _, _, sk), s in zip(heads, scores):
        m = jnp.maximum(jnp.max(_fold_lanes(s, jnp.maximum), axis=-1, keepdims=True), sk)
        p = jnp.exp2(s - m)
        denom = jnp.sum(_fold_lanes(p, jnp.add), axis=-1, keepdims=True) + jnp.exp2(sk - m)
        probs.append((p.astype(BF16), denom))
    for (cols, _, v, _), (p, denom) in zip(heads, probs):
        o = jnp.dot(p, v, preferred_element_type=F32)
        o_ref[:, cols] = (o / denom).astype(o_ref.dtype)


def _fold_lanes(x, op):
    tiles = [x[:, t * HEAD_DIM:(t + 1) * HEAD_DIM] for t in range(x.shape[1] // HEAD_DIM)]
    while len(tiles) > 1:
        tiles = [op(tiles[t], tiles[t + 1]) for t in range(0, len(tiles), 2)]
    return tiles[0]


def _rope(x, cos, sin_lo, sin_hi):
    return x * cos + pltpu.roll(x, 96, 1) * sin_lo + pltpu.roll(x, 32, 1) * sin_hi


def _attn_lat_kernel(sink_ref, q_ref, k_ref, v_ref, ck_ref, cv_ref, cos_ref, slo_ref, shi_ref,
                     cosq_ref, sloq_ref, shiq_ref, o_ref, kt_ref, vb_ref, ckt_ref, cvb_ref):
    kh = pl.program_id(1)
    n_blocks = LAT_LEN // ATTN_BLOCK
    kt = []
    for n in range(n_blocks):
        blk = slice(n * ATTN_BLOCK, (n + 1) * ATTN_BLOCK)
        kt.append(_rope(k_ref[blk, :], cos_ref[blk, :], slo_ref[blk, :], shi_ref[blk, :]).T.astype(BF16))
    pad = jnp.zeros((HEAD_DIM, ATTN_BLOCK), BF16)
    kt = [pad] + kt + [pad]
    for n in range(n_blocks):
        kt_ref[n] = jnp.concatenate(kt[n:n + 3], axis=1)
    vb_ref[:ATTN_BLOCK, :] = pad
    vb_ref[pl.ds(ATTN_BLOCK, LAT_LEN), :] = v_ref[...].astype(BF16)
    vb_ref[pl.ds(ATTN_BLOCK + LAT_LEN, ATTN_BLOCK), :] = pad
    ckt_ref[...] = ck_ref[...].T.astype(BF16)
    cvb_ref[...] = cv_ref[...].astype(BF16)

    rows = KV_GROUP * ATTN_BLOCK
    kq = (lax.broadcasted_iota(jnp.int32, (rows, ATTN_BLOCK), 1)
          - lax.broadcasted_iota(jnp.int32, (rows, ATTN_BLOCK), 0) % ATTN_BLOCK)
    sk = jnp.concatenate(
        [jnp.full((ATTN_BLOCK, 1), sink_ref[kh * KV_GROUP + g] * LOG2E, F32) for g in range(KV_GROUP)],
        axis=0)

    def scores(n):
        r0 = pl.multiple_of(n * ATTN_BLOCK, ATTN_BLOCK)
        cos = cosq_ref[pl.ds(r0, ATTN_BLOCK), :]
        slo = sloq_ref[pl.ds(r0, ATTN_BLOCK), :]
        shi = shiq_ref[pl.ds(r0, ATTN_BLOCK), :]
        q = jnp.concatenate(
            [_rope(q_ref[pl.ds(r0, ATTN_BLOCK), g * HEAD_DIM:(g + 1) * HEAD_DIM], cos, slo, shi).astype(BF16)
             for g in range(KV_GROUP)], axis=0)
        thr_p = jnp.where(n >= 1, 0, 2 * ATTN_BLOCK)
        thr_n = jnp.where(n <= n_blocks - 2, 0, -2 * ATTN_BLOCK)
        s_w = jnp.dot(q, kt_ref[n], preferred_element_type=F32)
        s_p = jnp.where(kq >= thr_p, s_w[:, :ATTN_BLOCK], NEG)
        s_0 = s_w[:, ATTN_BLOCK:2 * ATTN_BLOCK]
        s_n = jnp.where(kq <= thr_n, s_w[:, 2 * ATTN_BLOCK:], NEG)
        s_c = jnp.dot(q, ckt_ref[...], preferred_element_type=F32)
        return s_p, s_0, s_n, s_c

    def softmax(s_p, s_0, s_n, s_c):
        m = jnp.maximum(jnp.maximum(s_p, s_0), jnp.maximum(s_n, _fold_lanes(s_c, jnp.maximum)))
        m = jnp.maximum(jnp.max(m, axis=-1, keepdims=True), sk)
        p_p = jnp.exp2(s_p - m)
        p_0 = jnp.exp2(s_0 - m)
        p_n = jnp.exp2(s_n - m)
        p_c = jnp.exp2(s_c - m)
        denom = (jnp.sum((p_p + p_0) + (p_n + _fold_lanes(p_c, jnp.add)), axis=-1, keepdims=True)
                 + jnp.exp2(sk - m))
        p_w = jnp.concatenate([p_p.astype(BF16), p_0.astype(BF16), p_n.astype(BF16)], axis=1)
        return p_w, p_c.astype(BF16), denom

    def weighted_values(n, p_w, p_c, denom):
        r0 = pl.multiple_of(n * ATTN_BLOCK, ATTN_BLOCK)
        o = (jnp.dot(p_w, vb_ref[pl.ds(r0, 3 * ATTN_BLOCK), :], preferred_element_type=F32)
             + jnp.dot(p_c, cvb_ref[...], preferred_element_type=F32))
        o = o / denom
        for g in range(KV_GROUP):
            o_ref[pl.ds(r0, ATTN_BLOCK), g * HEAD_DIM:(g + 1) * HEAD_DIM] = (
                o[g * ATTN_BLOCK:(g + 1) * ATTN_BLOCK, :].astype(o_ref.dtype))

    def group(i, carry):
        blocks = [i * ATTN_GROUP + j for j in range(ATTN_GROUP)]
        s = [scores(n) for n in blocks]
        p = [softmax(*sn) for sn in s]
        for n, pn in zip(blocks, p):
            weighted_values(n, *pn)
        return carry

    lax.fori_loop(0, n_blocks // ATTN_GROUP, group, 0)


def _rope_tables():
    pos = jnp.arange(LAT_LEN, dtype=jnp.int32)
    row = (pos // GRID_W).astype(F32)
    col = (pos % GRID_W).astype(F32)
    half = HEAD_DIM // 2
    inv_freq = ROPE_BASE ** (-jnp.arange(0, half, 2, dtype=F32) / half)
    ang_r = row[:, None] * inv_freq[None, :]
    ang_c = col[:, None] * inv_freq[None, :]
    zero = jnp.zeros_like(ang_r)
    cos = jnp.concatenate([jnp.cos(ang_r), jnp.cos(ang_r), jnp.cos(ang_c), jnp.cos(ang_c)], axis=1)
    sin_lo = jnp.concatenate([-jnp.sin(ang_r), zero, -jnp.sin(ang_c), zero], axis=1)
    sin_hi = jnp.concatenate([zero, jnp.sin(ang_r), zero, jnp.sin(ang_c)], axis=1)
    return cos, sin_lo, sin_hi


def _attention(qkv, cache_k, cache_v, sink, layer_a):
    q_cols = KV_GROUP * HEAD_DIM
    k_blk0 = N_HEADS
    v_blk0 = N_HEADS + N_KV_HEADS
    ctx_rows0 = T_CTX // LAT_LEN
    cos, sin_lo, sin_hi = _rope_tables()
    q_tabs = [t * Q_SCALE for t in (cos, sin_lo, sin_hi)]
    ck = cache_k.reshape(N_LAT_REQ, -1, PAST_LEN, N_KV_HEADS * HEAD_DIM)
    cv = cache_v.reshape(N_LAT_REQ, -1, PAST_LEN, N_KV_HEADS * HEAD_DIM)
    tab_spec = pl.BlockSpec((LAT_LEN, HEAD_DIM), lambda b, h: (0, 0))
    smem_spec = pl.BlockSpec(memory_space=pltpu.SMEM)
    lat_block = LAT_LEN * HEAD_DIM * 4
    vmem_lat = (2 * (q_cols // HEAD_DIM + 2 + 1 + 6) * lat_block + 2 * LAT_LEN * q_cols * 2
                + 3 * lat_block + 16 * MIB)
    o_lat = pl.pallas_call(
        _attn_lat_kernel,
        grid=(N_LAT_REQ, N_KV_HEADS),
        in_specs=[
            smem_spec,
            pl.BlockSpec((LAT_LEN, q_cols), lambda b, h: (ctx_rows0 + b, h)),
            pl.BlockSpec((LAT_LEN, HEAD_DIM), lambda b, h: (ctx_rows0 + b, k_blk0 + h)),
            pl.BlockSpec((LAT_LEN, HEAD_DIM), lambda b, h: (ctx_rows0 + b, v_blk0 + h)),
            pl.BlockSpec((None, None, PAST_LEN, HEAD_DIM), lambda b, h: (b, layer_a, 0, h)),
            pl.BlockSpec((None, None, PAST_LEN, HEAD_DIM), lambda b, h: (b, layer_a, 0, h)),
            tab_spec, tab_spec, tab_spec, tab_spec, tab_spec, tab_spec,
        ],
        out_specs=pl.BlockSpec((LAT_LEN, q_cols), lambda b, h: (ctx_rows0 + b, h)),
        out_shape=jax.ShapeDtypeStruct((T_ALL, N_HEADS * HEAD_DIM), BF16),
        scratch_shapes=[pltpu.VMEM((LAT_LEN // ATTN_BLOCK, HEAD_DIM, 3 * ATTN_BLOCK), BF16),
                        pltpu.VMEM((LAT_LEN + 2 * ATTN_BLOCK, HEAD_DIM), BF16),
                        pltpu.VMEM((HEAD_DIM, PAST_LEN), BF16), pltpu.VMEM((PAST_LEN, HEAD_DIM), BF16)],
        compiler_params=_params(vmem_lat, 2),
        name="attn_latent",
    )(sink, qkv, qkv, qkv, ck, cv, cos, sin_lo, sin_hi, *q_tabs)

    cq = CTX_KV_PER_STEP * q_cols
    ckv = CTX_KV_PER_STEP * HEAD_DIM
    vmem_ctx = 2 * (CTX_LEN * cq * 4 + 2 * CTX_LEN * ckv * 4 + CTX_LEN * cq * 2) + 16 * MIB
    return pl.pallas_call(
        _attn_ctx_kernel,
        grid=(N_CTX_REQ, N_KV_HEADS // CTX_KV_PER_STEP),
        in_specs=[
            smem_spec,
            pl.BlockSpec((CTX_LEN, cq), lambda b, h: (b, h)),
            pl.BlockSpec((CTX_LEN, ckv), lambda b, h: (b, k_blk0 // CTX_KV_PER_STEP + h)),
            pl.BlockSpec((CTX_LEN, ckv), lambda b, h: (b, v_blk0 // CTX_KV_PER_STEP + h)),
            pl.BlockSpec(memory_space=pl.ANY),
        ],
        out_specs=pl.BlockSpec((CTX_LEN, cq), lambda b, h: (b, h)),
        out_shape=jax.ShapeDtypeStruct((T_ALL, N_HEADS * HEAD_DIM), BF16),
        input_output_aliases={4: 0},
        compiler_params=_params(vmem_ctx, 2),
        name="attn_context",
    )(sink, qkv, qkv, qkv, o_lat)


def _dft_tables(n):
    k = jnp.arange(n, dtype=jnp.int32)
    ang = ((k[:, None] * k[None, :]) % n).astype(F32) * (2.0 * math.pi / n)
    return jnp.cos(ang), jnp.sin(ang)


def _fourier_group(x, cs_ref, pos_ref, norm):
    z = jnp.dot(x, cs_ref[...], preferred_element_type=F32)
    zz = jnp.concatenate([z[:, :GROUP_DIM], z[:, GROUP_DIM:]], axis=0).astype(BF16)
    return jnp.dot(pos_ref[...], zz, preferred_element_type=F32) * norm


def _fourier_lat_kernel(x_ref, cs_ref, pos_ref, o_ref):
    norm = (LAT_LEN * GROUP_DIM) ** -0.5
    o_ref[...] = _fourier_group(x_ref[...], cs_ref, pos_ref, norm).astype(o_ref.dtype)


def _fourier_ctx_kernel(x_ref, cs_ref, pos_ref, prev_ref, o_ref):
    del prev_ref
    norm = (CTX_LEN * GROUP_DIM) ** -0.5
    for g in range(N_GROUPS):
        cols = slice(g * GROUP_DIM, (g + 1) * GROUP_DIM)
        o_ref[:, cols] = _fourier_group(x_ref[:, cols], cs_ref, pos_ref, norm).astype(o_ref.dtype)


def _fourier(h):
    cc, sc = _dft_tables(GROUP_DIM)
    cs = jnp.concatenate([cc, sc], axis=1).astype(BF16)
    cl, sl = _dft_tables(LAT_LEN)
    pos_lat = jnp.concatenate([cl, -sl], axis=1).astype(BF16)
    cp, sp = _dft_tables(CTX_LEN)
    pos_ctx = jnp.concatenate([cp, -sp], axis=1).astype(BF16)
    ctx_rows0 = T_CTX // LAT_LEN

    blk = LAT_LEN * GROUP_DIM
    vmem_lat = 2 * (2 * blk * 2 + GROUP_DIM * 2 * GROUP_DIM * 2 + LAT_LEN * 2 * LAT_LEN * 2) + 5 * blk * 4 + 4 * MIB
    f_lat = pl.pallas_call(
        _fourier_lat_kernel,
        grid=(N_LAT_REQ, N_GROUPS),
        in_specs=[pl.BlockSpec((LAT_LEN, GROUP_DIM), lambda b, g: (ctx_rows0 + b, g)),
                  pl.BlockSpec((GROUP_DIM, 2 * GROUP_DIM), lambda b, g: (0, 0)),
                  pl.BlockSpec((LAT_LEN, 2 * LAT_LEN), lambda b, g: (0, 0))],
        out_specs=pl.BlockSpec((LAT_LEN, GROUP_DIM), lambda b, g: (ctx_rows0 + b, g)),
        out_shape=jax.ShapeDtypeStruct((T_ALL, D_MODEL), BF16),
        compiler_params=_params(vmem_lat, 2),
        name="fourier_latent",
    )(h, cs, pos_lat)

    vmem_ctx = 2 * (2 * CTX_LEN * D_MODEL * 2 + GROUP_DIM * 2 * GROUP_DIM * 2 + CTX_LEN * 2 * CTX_LEN * 2) + 16 * MIB
    return pl.pallas_call(
        _fourier_ctx_kernel,
        grid=(N_CTX_REQ,),
        in_specs=[pl.BlockSpec((CTX_LEN, D_MODEL), lambda b: (b, 0)),
                  pl.BlockSpec((GROUP_DIM, 2 * GROUP_DIM), lambda b: (0, 0)),
                  pl.BlockSpec((CTX_LEN, 2 * CTX_LEN), lambda b: (0, 0)),
                  pl.BlockSpec(memory_space=pl.ANY)],
        out_specs=pl.BlockSpec((CTX_LEN, D_MODEL), lambda b: (b, 0)),
        out_shape=jax.ShapeDtypeStruct((T_ALL, D_MODEL), BF16),
        input_output_aliases={3: 0},
        compiler_params=_params(vmem_ctx, 1),
        name="fourier_context",
    )(h, cs, pos_ctx, f_lat)


def _cache_kernel(*refs):
    n_layers = (len(refs) - 2) // 2
    k_refs, v_refs = refs[:n_layers], refs[n_layers:2 * n_layers]
    ko_ref, vo_ref = refs[2 * n_layers:]
    for a in range(n_layers):
        @pl.when(pl.program_id(1) == a)
        def _(a=a):
            ko_ref[...] = k_refs[a][...].reshape(CTX_LEN, N_KV_HEADS, HEAD_DIM)
            vo_ref[...] = v_refs[a][...].reshape(CTX_LEN, N_KV_HEADS, HEAD_DIM)


def _new_caches(qkvs):
    n_layers = len(qkvs)
    kv_cols = N_KV_HEADS * HEAD_DIM
    k_blk = N_HEADS * HEAD_DIM // kv_cols
    shape = jax.ShapeDtypeStruct((N_CTX_REQ, n_layers, CTX_LEN, N_KV_HEADS, HEAD_DIM), F32)
    out_spec = pl.BlockSpec((None, None, CTX_LEN, N_KV_HEADS, HEAD_DIM), lambda b, a: (b, a, 0, 0, 0))
    vmem = 2 * (2 * n_layers + 2) * CTX_LEN * kv_cols * 4 + 4 * MIB
    return pl.pallas_call(
        _cache_kernel,
        grid=(N_CTX_REQ, n_layers),
        in_specs=([pl.BlockSpec((CTX_LEN, kv_cols), lambda b, a: (b, k_blk))] * n_layers
                  + [pl.BlockSpec((CTX_LEN, kv_cols), lambda b, a: (b, k_blk + 1))] * n_layers),
        out_specs=[out_spec, out_spec],
        out_shape=[shape, shape],
        compiler_params=_params(vmem, 2),
        name="new_caches",
    )(*qkvs, *qkvs)


def kernel(x_prompt, x_sample, cache_k, cache_v, c, c_ctx, w_ada, b_ada, g_mix_pre, g_mix_post,
           g_ffn_pre, g_ffn_post, w_qkv, w_attn_out, attn_sink, w_fourier, w_ffn_in, w_ffn_out):
    xp = x_prompt.reshape(T_CTX, D_MODEL)
    xs = x_sample.reshape(T_LAT, D_MODEL)
    cond = jnp.concatenate(
        [c, c_ctx[None, :], jnp.zeros((MOD_ROWS - N_LAT_REQ - 1, D_MODEL), F32)], axis=0)
    mod = _ada_table(cond, w_ada, b_ada).reshape(DEPTH, MOD_ROWS, 6, D_MODEL)
    g_mix_pre, g_mix_post, g_ffn_pre, g_ffn_post = (
        g.reshape(DEPTH, 1, D_MODEL) for g in (g_mix_pre, g_mix_post, g_ffn_pre, g_ffn_post))
    qkvs = []
    y = (xp, xs)
    h = _pre_call(xp, xs, g_mix_pre, mod, 0, 0)
    for i in range(DEPTH):
        if i % 2 == 0:
            a = i // 2
            qkv = _project(h, w_qkv, a, rows=MM_ROWS, cols=1024, chunk_rows=256, out_dtype=F32,
                           name="qkv_proj")
            qkvs.append(qkv)
            att = _attention(qkv, cache_k, cache_v, attn_sink[a], a)
            o = _project(att, w_attn_out, a, rows=MM_ROWS, cols=1024, chunk_rows=256, out_dtype=BF16,
                         name="attn_out_proj")
        else:
            o = _project(_fourier(h), w_fourier, i // 2, rows=MM_ROWS, cols=1024, chunk_rows=256,
                         out_dtype=BF16, name="fourier_proj")
        y, h = _postpre_call(o, y, g_mix_post, mod, i, 0, g_ffn_pre, i, 1)
        act = _swiglu_in(h, w_ffn_in, i)
        o = _project(act, w_ffn_out, i, rows=FFN_OUT_ROWS, cols=512, chunk_rows=256, out_dtype=BF16,
                     name="ffn_out_proj")
        if i + 1 < DEPTH:
            y, h = _postpre_call(o, y, g_ffn_post, mod, i, 1, g_mix_pre, i + 1, 0)
        else:
            y_p, y_s = _post_call(o, y, g_ffn_post, mod, i, 1)

    new_k, new_v = _new_caches(qkvs)
    return (y_p.reshape(N_CTX_REQ, CTX_LEN, D_MODEL), y_s.reshape(N_LAT_REQ, LAT_LEN, D_MODEL), new_k, new_v)
```

```python
import functools
import math

import jax
import jax.numpy as jnp
from jax import lax
from jax.experimental import pallas as pl
from jax.experimental.pallas import tpu as pltpu

F32 = jnp.float32
BF16 = jnp.bfloat16

D_MODEL = 4096
N_CTX_REQ = 16
CTX_LEN = 256
N_LAT_REQ = 8
LAT_LEN = 1024
PAST_LEN = 512
DEPTH = 4
GRID_W = 64
HEAD_DIM = 128
N_HEADS = 32
N_KV_HEADS = 8
KV_GROUP = 4
ATTN_BLOCK = 128
CTX_KV_PER_STEP = 4
ATTN_GROUP = 8
ROPE_BASE = 10000.0
N_GROUPS = 8
GROUP_DIM = D_MODEL // N_GROUPS
D_FF = 11008
EPS = 1e-6
NEG = -1e30

T_CTX = N_CTX_REQ * CTX_LEN
T_LAT = N_LAT_REQ * LAT_LEN
T_ALL = T_CTX + T_LAT
MOD_ROWS = 16
CTX_MOD_ROW = N_LAT_REQ
QKV_W = (N_HEADS + 2 * N_KV_HEADS) * HEAD_DIM
ATTN_SCALE = HEAD_DIM ** -0.5
LOG2E = math.log2(math.e)
Q_SCALE = ATTN_SCALE * LOG2E

V7X_VMEM_BYTES = 64 * 1024 * 1024
MIB = 1024 * 1024

ELEM_ROWS = 256
MM_ROWS = 1024
FFN_IN_ROWS = 2048
FFN_OUT_ROWS = 512
FFN_TILE = 256


def _params(vmem_bytes, ngrid):
    assert vmem_bytes <= V7X_VMEM_BYTES
    return pltpu.CompilerParams(
        dimension_semantics=("arbitrary",) * ngrid,
        vmem_limit_bytes=int(vmem_bytes),
    )


def _mod_row_of_block(i, rows_per_block):
    ctx_blocks = T_CTX // rows_per_block
    blocks_per_lat = LAT_LEN // rows_per_block
    return jnp.where(i < ctx_blocks, CTX_MOD_ROW, (i - ctx_blocks) // blocks_per_lat)


ADA_TILE = 512


def _ada_kernel(c_ref, w_ref, b_ref, o_ref):
    c = c_ref[...]
    s = (c * jax.nn.sigmoid(c)).astype(BF16)
    w = w_ref[0].astype(BF16)
    o_ref[0] = jnp.dot(s, w, preferred_element_type=F32) + b_ref[0]


def _ada_table(cond, w_ada, b_ada):
    n_out = 6 * D_MODEL
    vmem = 2 * (D_MODEL * ADA_TILE * 4) + D_MODEL * ADA_TILE * 2 + 4 * MIB
    return pl.pallas_call(
        _ada_kernel,
        grid=(DEPTH, n_out // ADA_TILE),
        in_specs=[
            pl.BlockSpec((MOD_ROWS, D_MODEL), lambda l, j: (0, 0)),
            pl.BlockSpec((1, D_MODEL, ADA_TILE), lambda l, j: (l, 0, j)),
            pl.BlockSpec((1, 1, ADA_TILE), lambda l, j: (l, 0, j)),
        ],
        out_specs=pl.BlockSpec((1, MOD_ROWS, ADA_TILE), lambda l, j: (l, 0, j)),
        out_shape=jax.ShapeDtypeStruct((DEPTH, MOD_ROWS, n_out), F32),
        compiler_params=_params(vmem, 2),
        name="ada_table",
    )(cond, w_ada, b_ada.reshape(DEPTH, 1, n_out))


def _normed(x):
    return x * lax.rsqrt(jnp.mean(x * x, axis=-1, keepdims=True) + EPS)


_CTX_BLOCKS = T_CTX // ELEM_ROWS
ELEM_CHUNK = 16


def _for_row_chunks(body):
    def step(c, carry):
        body(pl.ds(pl.multiple_of(c * ELEM_CHUNK, ELEM_CHUNK), ELEM_CHUNK))
        return carry
    lax.fori_loop(0, ELEM_ROWS // ELEM_CHUNK, step, 0, unroll=4)


def _rows(v):
    return jnp.broadcast_to(v, (ELEM_CHUNK, D_MODEL))


def _set_post_vec(rv_ref, k, gpost_ref, mod_ref, sub):
    rv_ref[k] = _rows(mod_ref[pl.ds(3 * sub + 2, 1), :] * gpost_ref[...])


def _set_pre_vecs(rv_ref, k, gpre_ref, mod_ref, sub):
    rv_ref[k] = _rows(gpre_ref[...] * (1.0 + mod_ref[pl.ds(3 * sub + 1, 1), :]))
    rv_ref[k + 1] = _rows(mod_ref[pl.ds(3 * sub, 1), :])


def _read_stream(yp_ref, ys_ref, rows):
    return jnp.where(pl.program_id(0) < _CTX_BLOCKS, yp_ref[rows, :], ys_ref[rows, :])


def _pre_kernel(sub, yp_ref, ys_ref, gpre_ref, mod_ref, h_ref, rv_ref):
    _set_pre_vecs(rv_ref, 0, gpre_ref, mod_ref, sub)

    def body(rows):
        h_ref[rows, :] = (_normed(_read_stream(yp_ref, ys_ref, rows)) * rv_ref[0] + rv_ref[1]).astype(h_ref.dtype)
    _for_row_chunks(body)


def _post_kernel(sub, o_ref, y_ref, gpost_ref, mod_ref, yp_ref, ys_ref, rv_ref):
    _set_post_vec(rv_ref, 0, gpost_ref, mod_ref, sub)

    def residual(rows):
        return y_ref[rows, :] + rv_ref[0] * _normed(o_ref[rows, :].astype(F32))

    @pl.when(pl.program_id(0) < _CTX_BLOCKS)
    def _():
        def body(rows):
            yp_ref[rows, :] = residual(rows)
        _for_row_chunks(body)

    @pl.when(pl.program_id(0) >= _CTX_BLOCKS)
    def _():
        def body(rows):
            ys_ref[rows, :] = residual(rows)
        _for_row_chunks(body)


def _postpre_kernel(sub, nsub, split_in, o_ref, *refs):
    if split_in:
        yp_ref, ys_ref, gpost_ref, mod_ref, gpre_ref, nmod_ref, yo_ref, h_ref, rv_ref = refs
    else:
        y_ref, gpost_ref, mod_ref, gpre_ref, nmod_ref, yo_ref, h_ref, rv_ref = refs
    _set_post_vec(rv_ref, 0, gpost_ref, mod_ref, sub)
    _set_pre_vecs(rv_ref, 1, gpre_ref, nmod_ref, nsub)

    def body(rows):
        y = _read_stream(yp_ref, ys_ref, rows) if split_in else y_ref[rows, :]
        yn = y + rv_ref[0] * _normed(o_ref[rows, :].astype(F32))
        yo_ref[rows, :] = yn
        h_ref[rows, :] = (_normed(yn) * rv_ref[1] + rv_ref[2]).astype(h_ref.dtype)
    _for_row_chunks(body)


def _rv_scratch(n):
    return [pltpu.VMEM((n, ELEM_CHUNK, D_MODEL), F32)]


def _row_spec():
    return pl.BlockSpec((ELEM_ROWS, D_MODEL), lambda i: (i, 0))


def _split_specs():
    return [pl.BlockSpec((ELEM_ROWS, D_MODEL), lambda i: (jnp.minimum(i, _CTX_BLOCKS - 1), 0)),
            pl.BlockSpec((ELEM_ROWS, D_MODEL), lambda i: (jnp.maximum(i - _CTX_BLOCKS, 0), 0))]


def _gain_spec(layer):
    return pl.BlockSpec((None, 1, D_MODEL), lambda i: (layer, 0, 0))


def _mod_spec(layer):
    return pl.BlockSpec((None, None, 6, D_MODEL),
                        lambda i: (layer, _mod_row_of_block(i, ELEM_ROWS), 0, 0))


_ELEM_BLOCK_F32 = ELEM_ROWS * D_MODEL * 4


def _pre_call(xp, xs, g_pre, mod, layer, sub):
    vmem = 2 * (2 * _ELEM_BLOCK_F32 + _ELEM_BLOCK_F32 // 2) + 3 * _ELEM_BLOCK_F32 + 2 * MIB
    return pl.pallas_call(
        functools.partial(_pre_kernel, sub),
        grid=(T_ALL // ELEM_ROWS,),
        in_specs=_split_specs() + [_gain_spec(layer), _mod_spec(layer)],
        out_specs=_row_spec(),
        out_shape=jax.ShapeDtypeStruct((T_ALL, D_MODEL), BF16),
        scratch_shapes=_rv_scratch(2),
        compiler_params=_params(vmem, 1),
        name="pre_norm",
    )(xp, xs, g_pre, mod)


def _post_call(o, y, g_post, mod, layer, sub):
    vmem = 2 * 4 * _ELEM_BLOCK_F32 + 3 * _ELEM_BLOCK_F32 + 2 * MIB
    return pl.pallas_call(
        functools.partial(_post_kernel, sub),
        grid=(T_ALL // ELEM_ROWS,),
        in_specs=[_row_spec(), _row_spec(), _gain_spec(layer), _mod_spec(layer)],
        out_specs=_split_specs(),
        out_shape=[jax.ShapeDtypeStruct((T_CTX, D_MODEL), F32),
                   jax.ShapeDtypeStruct((T_LAT, D_MODEL), F32)],
        scratch_shapes=_rv_scratch(1),
        compiler_params=_params(vmem, 1),
        name="post_norm",
    )(o, y, g_post, mod)


def _postpre_call(o, y, g_post, mod, layer, sub, g_pre, nlayer, nsub):
    split_in = isinstance(y, tuple)
    y_args = y if split_in else (y,)
    y_specs = _split_specs() if split_in else [_row_spec()]
    vmem = 2 * ((2 + len(y_args)) * _ELEM_BLOCK_F32 + _ELEM_BLOCK_F32 // 2) + 4 * _ELEM_BLOCK_F32 + 2 * MIB
    return pl.pallas_call(
        functools.partial(_postpre_kernel, sub, nsub, split_in),
        grid=(T_ALL // ELEM_ROWS,),
        in_specs=[_row_spec()] + y_specs + [_gain_spec(layer), _mod_spec(layer),
                                            _gain_spec(nlayer), _mod_spec(nlayer)],
        out_specs=[_row_spec(), _row_spec()],
        out_shape=[jax.ShapeDtypeStruct((T_ALL, D_MODEL), F32),
                   jax.ShapeDtypeStruct((T_ALL, D_MODEL), BF16)],
        scratch_shapes=_rv_scratch(3),
        compiler_params=_params(vmem, 1),
        name="post_pre_norm",
    )(o, *y_args, g_post, mod, g_pre, mod)


STREAM_SLOTS = 2


def _aligned(v, m):
    return v if isinstance(v, int) else pl.multiple_of(v, m)


def _wstream_kernel(layer, seg_starts, seg_width, chunk_rows, n_chunks, epilogue,
                    x_ref, w_hbm, o_ref, wb0_ref, wb1_ref, stage_ref, sem):
    j = pl.program_id(0)
    i = pl.program_id(1)
    nj = pl.num_programs(0)
    ni = pl.num_programs(1)
    step = j * ni + i
    set_new = lax.rem(step, 2)
    set_old = 1 - set_new

    def chunk_copies(tile, c, set_, slot):
        r0 = _aligned(c * chunk_rows, chunk_rows)
        return [pltpu.make_async_copy(
            w_hbm.at[layer, pl.ds(r0, chunk_rows), pl.ds(_aligned(start(tile), seg_width), seg_width)],
            stage_ref.at[set_, slot, :, pl.ds(s * seg_width, seg_width)],
            sem.at[set_, slot]) for s, start in enumerate(seg_starts)]

    def group_chunk(group, slot):
        return jnp.minimum(group * STREAM_SLOTS + slot, n_chunks - 1)

    def start_group(tile, group, set_):
        for slot in range(STREAM_SLOTS):
            for cp in chunk_copies(tile, group_chunk(group, slot), set_, slot):
                cp.start()

    def round_chunk(wb_ref, c, set_, slot):
        wb_ref[pl.ds(_aligned(c * chunk_rows, chunk_rows), chunk_rows), :] = stage_ref[set_, slot].astype(BF16)

    @pl.when(step == 0)
    def _():
        for cp in chunk_copies(0, 0, 0, 0):
            cp.start()
        for c in range(n_chunks):
            if c + 1 < n_chunks:
                for cp in chunk_copies(0, c + 1, (c + 1) % 2, 0):
                    cp.start()
            for cp in chunk_copies(0, c, c % 2, 0):
                cp.wait()
            round_chunk(wb0_ref, c, c % 2, 0)
        start_group(jnp.minimum(1, nj - 1), 0, 1)

    for slot in range(STREAM_SLOTS):
        for cp in chunk_copies(0, 0, set_old, slot):
            cp.wait()

    @pl.when(step < nj * ni - 1)
    def _():
        wrap = i == ni - 1
        tile = jnp.minimum(jnp.where(wrap, j + 2, j + 1), nj - 1)
        start_group(tile, jnp.where(wrap, 0, i + 1), set_new)

    for parity, (wb_cur, wb_idle) in enumerate(((wb0_ref, wb1_ref), (wb1_ref, wb0_ref))):
        @pl.when(lax.rem(j, 2) == parity)
        def _(wb_cur=wb_cur, wb_idle=wb_idle):
            for slot in range(STREAM_SLOTS):
                round_chunk(wb_idle, group_chunk(i, slot), set_old, slot)
            acc = jnp.dot(x_ref[...], wb_cur[...], preferred_element_type=F32)
            o_ref[...] = epilogue(acc).astype(o_ref.dtype)


def _matmul_wstream(x, w_stack, layer, *, rows, seg_starts, seg_width, out_cols, chunk_rows, out_dtype,
                    epilogue, name):
    m, k = x.shape
    cols = len(seg_starts) * seg_width
    n_tiles = w_stack.shape[2] // cols
    n_row_blocks = m // rows
    n_chunks = k // chunk_rows
    assert m % rows == 0 and k % chunk_rows == 0 and w_stack.shape[2] % cols == 0
    assert n_row_blocks * STREAM_SLOTS >= n_chunks >= 2
    out_bytes = jnp.dtype(out_dtype).itemsize
    vmem = (2 * rows * k * 2 + 2 * k * cols * 2 + 2 * STREAM_SLOTS * chunk_rows * cols * 4
            + 2 * rows * out_cols * out_bytes + 2 * rows * cols * 4 + 2 * MIB)
    return pl.pallas_call(
        functools.partial(_wstream_kernel, layer, seg_starts, seg_width, chunk_rows, n_chunks, epilogue),
        grid=(n_tiles, n_row_blocks),
        in_specs=[pl.BlockSpec((rows, k), lambda j, i: (i, 0)),
                  pl.BlockSpec(memory_space=pl.ANY)],
        out_specs=pl.BlockSpec((rows, out_cols), lambda j, i: (i, j)),
        out_shape=jax.ShapeDtypeStruct((m, n_tiles * out_cols), out_dtype),
        scratch_shapes=[pltpu.VMEM((k, cols), BF16), pltpu.VMEM((k, cols), BF16),
                        pltpu.VMEM((2, STREAM_SLOTS, chunk_rows, cols), F32),
                        pltpu.SemaphoreType.DMA((2, STREAM_SLOTS))],
        compiler_params=_params(vmem, 2),
        name=name,
    )(x, w_stack)


def _identity(acc):
    return acc


def _project(x, w_stack, layer, *, rows, cols, chunk_rows, out_dtype, name):
    return _matmul_wstream(x, w_stack, layer, rows=rows, seg_starts=[lambda t: t * cols], seg_width=cols,
                           out_cols=cols, chunk_rows=chunk_rows, out_dtype=out_dtype, epilogue=_identity,
                           name=name)


def _swiglu(acc):
    g = acc[:, :FFN_TILE]
    u = acc[:, FFN_TILE:]
    return g * jax.nn.sigmoid(g) * u


def _swiglu_in(h, w_in_stack, layer):
    return _matmul_wstream(
        h, w_in_stack, layer, rows=FFN_IN_ROWS,
        seg_starts=[lambda t: t * FFN_TILE, lambda t: D_FF + t * FFN_TILE], seg_width=FFN_TILE,
        out_cols=FFN_TILE, chunk_rows=512, out_dtype=BF16, epilogue=_swiglu, name="ffn_in_swiglu")


def _attn_ctx_kernel(sink_ref, q_ref, k_ref, v_ref, prev_ref, o_ref):
    del prev_ref
    head0 = pl.program_id(1) * (CTX_KV_PER_STEP * KV_GROUP)
    heads = []
    for kl in range(CTX_KV_PER_STEP):
        kv_cols = slice(kl * HEAD_DIM, (kl + 1) * HEAD_DIM)
        kt = k_ref[:, kv_cols].T.astype(BF16)
        v = v_ref[:, kv_cols].astype(BF16)
        for g in range(KV_GROUP):
            hl = kl * KV_GROUP + g
            heads.append((slice(hl * HEAD_DIM, (hl + 1) * HEAD_DIM), kt, v, sink_ref[head0 + hl] * LOG2E))
    scores = [jnp.dot((q_ref[:, cols] * Q_SCALE).astype(BF16), kt, preferred_element_type=F32)
              for cols, kt, _, _ in heads]
    probs = []
    for (_, _, _, sk), s in zip(heads, scores):
        m = jnp.maximum(jnp.max(_fold_lanes(s, jnp.maximum), axis=-1, keepdims=True), sk)
        p = jnp.exp2(s - m)
        denom = jnp.sum(_fold_lanes(p, jnp.add), axis=-1, keepdims=True) + jnp.exp2(sk - m)
        probs.append((p.astype(BF16), denom))
    for (cols, _, v, _), (p, denom) in zip(heads, probs):
        o = jnp.dot(p, v, preferred_element_type=F32)
        o_ref[:, cols] = (o / denom).astype(o_ref.dtype)


def _fold_lanes(x, op):
    tiles = [x[:, t * HEAD_DIM:(t + 1) * HEAD_DIM] for t in range(x.shape[1] // HEAD_DIM)]
    while len(tiles) > 1:
        tiles = [op(tiles[t], tiles[t + 1]) for t in range(0, len(tiles), 2)]
    return tiles[0]


def _rope(x, cos, sin_lo, sin_hi):
    return x * cos + pltpu.roll(x, 96, 1) * sin_lo + pltpu.roll(x, 32, 1) * sin_hi


def _attn_lat_kernel(sink_ref, q_ref, k_ref, v_ref, ck_ref, cv_ref, cos_ref, slo_ref, shi_ref,
                     cosq_ref, sloq_ref, shiq_ref, o_ref, kt_ref, vb_ref, ckt_ref, cvb_ref):
    kh = pl.program_id(1)
    n_blocks = LAT_LEN // ATTN_BLOCK
    kt = []
    for n in range(n_blocks):
        blk = slice(n * ATTN_BLOCK, (n + 1) * ATTN_BLOCK)
        kt.append(_rope(k_ref[blk, :], cos_ref[blk, :], slo_ref[blk, :], shi_ref[blk, :]).T.astype(BF16))
    pad = jnp.zeros((HEAD_DIM, ATTN_BLOCK), BF16)
    kt = [pad] + kt + [pad]
    for n in range(n_blocks):
        kt_ref[n] = jnp.concatenate(kt[n:n + 3], axis=1)
    vb_ref[:ATTN_BLOCK, :] = pad
    vb_ref[pl.ds(ATTN_BLOCK, LAT_LEN), :] = v_ref[...].astype(BF16)
    vb_ref[pl.ds(ATTN_BLOCK + LAT_LEN, ATTN_BLOCK), :] = pad
    head_rows = pl.ds(kh, PAST_LEN, stride=N_KV_HEADS)
    ckt_ref[...] = ck_ref[head_rows, :].T.astype(BF16)
    cvb_ref[...] = cv_ref[head_rows, :].astype(BF16)

    rows = KV_GROUP * ATTN_BLOCK
    kq = (lax.broadcasted_iota(jnp.int32, (rows, ATTN_BLOCK), 1)
          - lax.broadcasted_iota(jnp.int32, (rows, ATTN_BLOCK), 0) % ATTN_BLOCK)
    sk = jnp.concatenate(
        [jnp.full((ATTN_BLOCK, 1), sink_ref[kh * KV_GROUP + g] * LOG2E, F32) for g in range(KV_GROUP)],
        axis=0)

    def scores(n):
        r0 = pl.multiple_of(n * ATTN_BLOCK, ATTN_BLOCK)
        cos = cosq_ref[pl.ds(r0, ATTN_BLOCK), :]
        slo = sloq_ref[pl.ds(r0, ATTN_BLOCK), :]
        shi = shiq_ref[pl.ds(r0, ATTN_BLOCK), :]
        q = jnp.concatenate(
            [_rope(q_ref[pl.ds(r0, ATTN_BLOCK), g * HEAD_DIM:(g + 1) * HEAD_DIM], cos, slo, shi).astype(BF16)
             for g in range(KV_GROUP)], axis=0)
        thr_p = jnp.where(n >= 1, 0, 2 * ATTN_BLOCK)
        thr_n = jnp.where(n <= n_blocks - 2, 0, -2 * ATTN_BLOCK)
        s_w = jnp.dot(q, kt_ref[n], preferred_element_type=F32)
        s_p = jnp.where(kq >= thr_p, s_w[:, :ATTN_BLOCK], NEG)
        s_0 = s_w[:, ATTN_BLOCK:2 * ATTN_BLOCK]
        s_n = jnp.where(kq <= thr_n, s_w[:, 2 * ATTN_BLOCK:], NEG)
        s_c = jnp.dot(q, ckt_ref[...], preferred_element_type=F32)
        return s_p, s_0, s_n, s_c

    def softmax(s_p, s_0, s_n, s_c):
        m = jnp.maximum(jnp.maximum(s_p, s_0), jnp.maximum(s_n, _fold_lanes(s_c, jnp.maximum)))
        m = jnp.maximum(jnp.max(m, axis=-1, keepdims=True), sk)
        p_p = jnp.exp2(s_p - m)
        p_0 = jnp.exp2(s_0 - m)
        p_n = jnp.exp2(s_n - m)
        p_c = jnp.exp2(s_c - m)
        denom = (jnp.sum((p_p + p_0) + (p_n + _fold_lanes(p_c, jnp.add)), axis=-1, keepdims=True)
                 + jnp.exp2(sk - m))
        p_w = jnp.concatenate([p_p.astype(BF16), p_0.astype(BF16), p_n.astype(BF16)], axis=1)
        return p_w, p_c.astype(BF16), denom

    def weighted_values(n, p_w, p_c, denom):
        r0 = pl.multiple_of(n * ATTN_BLOCK, ATTN_BLOCK)
        o = (jnp.dot(p_w, vb_ref[pl.ds(r0, 3 * ATTN_BLOCK), :], preferred_element_type=F32)
             + jnp.dot(p_c, cvb_ref[...], preferred_element_type=F32))
        o = o / denom
        for g in range(KV_GROUP):
            o_ref[pl.ds(r0, ATTN_BLOCK), g * HEAD_DIM:(g + 1) * HEAD_DIM] = (
                o[g * ATTN_BLOCK:(g + 1) * ATTN_BLOCK, :].astype(o_ref.dtype))

    def group(i, carry):
        blocks = [i * ATTN_GROUP + j for j in range(ATTN_GROUP)]
        s = [scores(n) for n in blocks]
        p = [softmax(*sn) for sn in s]
        for n, pn in zip(blocks, p):
            weighted_values(n, *pn)
        return carry

    lax.fori_loop(0, n_blocks // ATTN_GROUP, group, 0)


def _rope_tables():
    pos = jnp.arange(LAT_LEN, dtype=jnp.int32)
    row = (pos // GRID_W).astype(F32)
    col = (pos % GRID_W).astype(F32)
    half = HEAD_DIM // 2
    inv_freq = ROPE_BASE ** (-jnp.arange(0, half, 2, dtype=F32) / half)
    ang_r = row[:, None] * inv_freq[None, :]
    ang_c = col[:, None] * inv_freq[None, :]
    zero = jnp.zeros_like(ang_r)
    cos = jnp.concatenate([jnp.cos(ang_r), jnp.cos(ang_r), jnp.cos(ang_c), jnp.cos(ang_c)], axis=1)
    sin_lo = jnp.concatenate([-jnp.sin(ang_r), zero, -jnp.sin(ang_c), zero], axis=1)
    sin_hi = jnp.concatenate([zero, jnp.sin(ang_r), zero, jnp.sin(ang_c)], axis=1)
    return cos, sin_lo, sin_hi


def _attention(qkv, cache_k, cache_v, sink, layer_a):
    q_cols = KV_GROUP * HEAD_DIM
    k_blk0 = N_HEADS
    v_blk0 = N_HEADS + N_KV_HEADS
    ctx_rows0 = T_CTX // LAT_LEN
    cos, sin_lo, sin_hi = _rope_tables()
    q_tabs = [t * Q_SCALE for t in (cos, sin_lo, sin_hi)]
    ck = cache_k.reshape(N_LAT_REQ, -1, PAST_LEN * N_KV_HEADS, HEAD_DIM)
    cv = cache_v.reshape(N_LAT_REQ, -1, PAST_LEN * N_KV_HEADS, HEAD_DIM)
    cache_spec = pl.BlockSpec((None, None, PAST_LEN * N_KV_HEADS, HEAD_DIM), lambda b, h: (b, layer_a, 0, 0))
    tab_spec = pl.BlockSpec((LAT_LEN, HEAD_DIM), lambda b, h: (0, 0))
    smem_spec = pl.BlockSpec(memory_space=pltpu.SMEM)
    lat_block = LAT_LEN * HEAD_DIM * 4
    cache_blocks = 2 * PAST_LEN * N_KV_HEADS // LAT_LEN
    vmem_lat = (2 * (q_cols // HEAD_DIM + 2 + cache_blocks + 6) * lat_block + 2 * LAT_LEN * q_cols * 2
                + 3 * lat_block + 16 * MIB)
    o_lat = pl.pallas_call(
        _attn_lat_kernel,
        grid=(N_LAT_REQ, N_KV_HEADS),
        in_specs=[
            smem_spec,
            pl.BlockSpec((LAT_LEN, q_cols), lambda b, h: (ctx_rows0 + b, h)),
            pl.BlockSpec((LAT_LEN, HEAD_DIM), lambda b, h: (ctx_rows0 + b, k_blk0 + h)),
            pl.BlockSpec((LAT_LEN, HEAD_DIM), lambda b, h: (ctx_rows0 + b, v_blk0 + h)),
            cache_spec, cache_spec,
            tab_spec, tab_spec, tab_spec, tab_spec, tab_spec, tab_spec,
        ],
        out_specs=pl.BlockSpec((LAT_LEN, q_cols), lambda b, h: (ctx_rows0 + b, h)),
        out_shape=jax.ShapeDtypeStruct((T_ALL, N_HEADS * HEAD_DIM), BF16),
        scratch_shapes=[pltpu.VMEM((LAT_LEN // ATTN_BLOCK, HEAD_DIM, 3 * ATTN_BLOCK), BF16),
                        pltpu.VMEM((LAT_LEN + 2 * ATTN_BLOCK, HEAD_DIM), BF16),
                        pltpu.VMEM((HEAD_DIM, PAST_LEN), BF16), pltpu.VMEM((PAST_LEN, HEAD_DIM), BF16)],
        compiler_params=_params(vmem_lat, 2),
        name="attn_latent",
    )(sink, qkv, qkv, qkv, ck, cv, cos, sin_lo, sin_hi, *q_tabs)

    cq = CTX_KV_PER_STEP * q_cols
    ckv = CTX_KV_PER_STEP * HEAD_DIM
    vmem_ctx = 2 * (CTX_LEN * cq * 4 + 2 * CTX_LEN * ckv * 4 + CTX_LEN * cq * 2) + 16 * MIB
    return pl.pallas_call(
        _attn_ctx_kernel,
        grid=(N_CTX_REQ, N_KV_HEADS // CTX_KV_PER_STEP),
        in_specs=[
            smem_spec,
            pl.BlockSpec((CTX_LEN, cq), lambda b, h: (b, h)),
            pl.BlockSpec((CTX_LEN, ckv), lambda b, h: (b, k_blk0 // CTX_KV_PER_STEP + h)),
            pl.BlockSpec((CTX_LEN, ckv), lambda b, h: (b, v_blk0 // CTX_KV_PER_STEP + h)),
            pl.BlockSpec(memory_space=pl.ANY),
        ],
        out_specs=pl.BlockSpec((CTX_LEN, cq), lambda b, h: (b, h)),
        out_shape=jax.ShapeDtypeStruct((T_ALL, N_HEADS * HEAD_DIM), BF16),
        input_output_aliases={4: 0},
        compiler_params=_params(vmem_ctx, 2),
        name="attn_context",
    )(sink, qkv, qkv, qkv, o_lat)


def _dft_tables(n):
    k = jnp.arange(n, dtype=jnp.int32)
    ang = ((k[:, None] * k[None, :]) % n).astype(F32) * (2.0 * math.pi / n)
    return jnp.cos(ang), jnp.sin(ang)


def _fourier_group(x, cs_ref, pos_ref, norm):
    z = jnp.dot(x, cs_ref[...], preferred_element_type=F32)
    zz = jnp.concatenate([z[:, :GROUP_DIM], z[:, GROUP_DIM:]], axis=0).astype(BF16)
    return jnp.dot(pos_ref[...], zz, preferred_element_type=F32) * norm


def _fourier_lat_kernel(x_ref, cs_ref, pos_ref, o_ref):
    norm = (LAT_LEN * GROUP_DIM) ** -0.5
    o_ref[...] = _fourier_group(x_ref[...], cs_ref, pos_ref, norm).astype(o_ref.dtype)


def _fourier_ctx_kernel(x_ref, cs_ref, pos_ref, prev_ref, o_ref):
    del prev_ref
    norm = (CTX_LEN * GROUP_DIM) ** -0.5
    for g in range(N_GROUPS):
        cols = slice(g * GROUP_DIM, (g + 1) * GROUP_DIM)
        o_ref[:, cols] = _fourier_group(x_ref[:, cols], cs_ref, pos_ref, norm).astype(o_ref.dtype)


def _fourier(h):
    cc, sc = _dft_tables(GROUP_DIM)
    cs = jnp.concatenate([cc, sc], axis=1).astype(BF16)
    cl, sl = _dft_tables(LAT_LEN)
    pos_lat = jnp.concatenate([cl, -sl], axis=1).astype(BF16)
    cp, sp = _dft_tables(CTX_LEN)
    pos_ctx = jnp.concatenate([cp, -sp], axis=1).astype(BF16)
    ctx_rows0 = T_CTX // LAT_LEN

    blk = LAT_LEN * GROUP_DIM
    vmem_lat = 2 * (2 * blk * 2 + GROUP_DIM * 2 * GROUP_DIM * 2 + LAT_LEN * 2 * LAT_LEN * 2) + 5 * blk * 4 + 4 * MIB
    f_lat = pl.pallas_call(
        _fourier_lat_kernel,
        grid=(N_LAT_REQ, N_GROUPS),
        in_specs=[pl.BlockSpec((LAT_LEN, GROUP_DIM), lambda b, g: (ctx_rows0 + b, g)),
                  pl.BlockSpec((GROUP_DIM, 2 * GROUP_DIM), lambda b, g: (0, 0)),
                  pl.BlockSpec((LAT_LEN, 2 * LAT_LEN), lambda b, g: (0, 0))],
        out_specs=pl.BlockSpec((LAT_LEN, GROUP_DIM), lambda b, g: (ctx_rows0 + b, g)),
        out_shape=jax.ShapeDtypeStruct((T_ALL, D_MODEL), BF16),
        compiler_params=_params(vmem_lat, 2),
        name="fourier_latent",
    )(h, cs, pos_lat)

    vmem_ctx = 2 * (2 * CTX_LEN * D_MODEL * 2 + GROUP_DIM * 2 * GROUP_DIM * 2 + CTX_LEN * 2 * CTX_LEN * 2) + 16 * MIB
    return pl.pallas_call(
        _fourier_ctx_kernel,
        grid=(N_CTX_REQ,),
        in_specs=[pl.BlockSpec((CTX_LEN, D_MODEL), lambda b: (b, 0)),
                  pl.BlockSpec((GROUP_DIM, 2 * GROUP_DIM), lambda b: (0, 0)),
                  pl.BlockSpec((CTX_LEN, 2 * CTX_LEN), lambda b: (0, 0)),
                  pl.BlockSpec(memory_space=pl.ANY)],
        out_specs=pl.BlockSpec((CTX_LEN, D_MODEL), lambda b: (b, 0)),
        out_shape=jax.ShapeDtypeStruct((T_ALL, D_MODEL), BF16),
        input_output_aliases={3: 0},
        compiler_params=_params(vmem_ctx, 1),
        name="fourier_context",
    )(h, cs, pos_ctx, f_lat)


def _cache_kernel(*refs):
    n_layers = (len(refs) - 2) // 2
    k_refs, v_refs = refs[:n_layers], refs[n_layers:2 * n_layers]
    ko_ref, vo_ref = refs[2 * n_layers:]
    for a in range(n_layers):
        @pl.when(pl.program_id(1) == a)
        def _(a=a):
            ko_ref[...] = k_refs[a][...].reshape(CTX_LEN, N_KV_HEADS, HEAD_DIM)
            vo_ref[...] = v_refs[a][...].reshape(CTX_LEN, N_KV_HEADS, HEAD_DIM)


def _new_caches(qkvs):
    n_layers = len(qkvs)
    kv_cols = N_KV_HEADS * HEAD_DIM
    k_blk = N_HEADS * HEAD_DIM // kv_cols
    shape = jax.ShapeDtypeStruct((N_CTX_REQ, n_layers, CTX_LEN, N_KV_HEADS, HEAD_DIM), F32)
    out_spec = pl.BlockSpec((None, None, CTX_LEN, N_KV_HEADS, HEAD_DIM), lambda b, a: (b, a, 0, 0, 0))
    vmem = 2 * (2 * n_layers + 2) * CTX_LEN * kv_cols * 4 + 4 * MIB
    return pl.pallas_call(
        _cache_kernel,
        grid=(N_CTX_REQ, n_layers),
        in_specs=([pl.BlockSpec((CTX_LEN, kv_cols), lambda b, a: (b, k_blk))] * n_layers
                  + [pl.BlockSpec((CTX_LEN, kv_cols), lambda b, a: (b, k_blk + 1))] * n_layers),
        out_specs=[out_spec, out_spec],
        out_shape=[shape, shape],
        compiler_params=_params(vmem, 2),
        name="new_caches",
    )(*qkvs, *qkvs)


def kernel(x_prompt, x_sample, cache_k, cache_v, c, c_ctx, w_ada, b_ada, g_mix_pre, g_mix_post,
           g_ffn_pre, g_ffn_post, w_qkv, w_attn_out, attn_sink, w_fourier, w_ffn_in, w_ffn_out):
    xp = x_prompt.reshape(T_CTX, D_MODEL)
    xs = x_sample.reshape(T_LAT, D_MODEL)
    cond = jnp.concatenate(
        [c, c_ctx[None, :], jnp.zeros((MOD_ROWS - N_LAT_REQ - 1, D_MODEL), F32)], axis=0)
    mod = _ada_table(cond, w_ada, b_ada).reshape(DEPTH, MOD_ROWS, 6, D_MODEL)
    g_mix_pre, g_mix_post, g_ffn_pre, g_ffn_post = (
        g.reshape(DEPTH, 1, D_MODEL) for g in (g_mix_pre, g_mix_post, g_ffn_pre, g_ffn_post))
    qkvs = []
    y = (xp, xs)
    h = _pre_call(xp, xs, g_mix_pre, mod, 0, 0)
    for i in range(DEPTH):
        if i % 2 == 0:
            a = i // 2
            qkv = _project(h, w_qkv, a, rows=MM_ROWS, cols=1024, chunk_rows=256, out_dtype=F32,
                           name="qkv_proj")
            qkvs.append(qkv)
            att = _attention(qkv, cache_k, cache_v, attn_sink[a], a)
            o = _project(att, w_attn_out, a, rows=MM_ROWS, cols=1024, chunk_rows=256, out_dtype=BF16,
                         name="attn_out_proj")
        else:
            o = _project(_fourier(h), w_fourier, i // 2, rows=MM_ROWS, cols=1024, chunk_rows=256,
                         out_dtype=BF16, name="fourier_proj")
        y, h = _postpre_call(o, y, g_mix_post, mod, i, 0, g_ffn_pre, i, 1)
        act = _swiglu_in(h, w_ffn_in, i)
        o = _project(act, w_ffn_out, i, rows=FFN_OUT_ROWS, cols=512, chunk_rows=256, out_dtype=BF16,
                     name="ffn_out_proj")
        if i + 1 < DEPTH:
            y, h = _postpre_call(o, y, g_ffn_post, mod, i, 1, g_mix_pre, i + 1, 0)
        else:
            y_p, y_s = _post_call(o, y, g_ffn_post, mod, i, 1)

    new_k, new_v = _new_caches(qkvs)
    return (y_p.reshape(N_CTX_REQ, CTX_LEN, D_MODEL), y_s.reshape(N_LAT_REQ, LAT_LEN, D_MODEL), new_k, new_v)
```

```python
import functools
import math

import jax
import jax.numpy as jnp
from jax import lax
from jax.experimental import pallas as pl
from jax.experimental.pallas import tpu as pltpu

F32 = jnp.float32
BF16 = jnp.bfloat16

D_MODEL = 4096
N_CTX_REQ = 16
CTX_LEN = 256
N_LAT_REQ = 8
LAT_LEN = 1024
PAST_LEN = 512
DEPTH = 4
GRID_W = 64
HEAD_DIM = 128
N_HEADS = 32
N_KV_HEADS = 8
KV_GROUP = 4
ATTN_BLOCK = 128
CTX_KV_PER_STEP = 4
ATTN_GROUP = 8
ROPE_BASE = 10000.0
N_GROUPS = 8
GROUP_DIM = D_MODEL // N_GROUPS
D_FF = 11008
EPS = 1e-6
NEG = -1e30

T_CTX = N_CTX_REQ * CTX_LEN
T_LAT = N_LAT_REQ * LAT_LEN
T_ALL = T_CTX + T_LAT
MOD_ROWS = 16
CTX_MOD_ROW = N_LAT_REQ
QKV_W = (N_HEADS + 2 * N_KV_HEADS) * HEAD_DIM
ATTN_SCALE = HEAD_DIM ** -0.5
LOG2E = math.log2(math.e)
Q_SCALE = ATTN_SCALE * LOG2E

V7X_VMEM_BYTES = 64 * 1024 * 1024
MIB = 1024 * 1024

ELEM_ROWS = 256
MM_ROWS = 1024
FFN_IN_ROWS = 2048
FFN_OUT_ROWS = 512
FFN_TILE = 256


def _params(vmem_bytes, ngrid):
    assert vmem_bytes <= V7X_VMEM_BYTES
    return pltpu.CompilerParams(
        dimension_semantics=("arbitrary",) * ngrid,
        vmem_limit_bytes=int(vmem_bytes),
    )


def _mod_row_of_block(i, rows_per_block):
    ctx_blocks = T_CTX // rows_per_block
    blocks_per_lat = LAT_LEN // rows_per_block
    return jnp.where(i < ctx_blocks, CTX_MOD_ROW, (i - ctx_blocks) // blocks_per_lat)


ADA_TILE = 512


def _ada_kernel(c_ref, w_ref, b_ref, o_ref):
    c = c_ref[...]
    s = (c * jax.nn.sigmoid(c)).astype(BF16)
    w = w_ref[0].astype(BF16)
    o_ref[0] = jnp.dot(s, w, preferred_element_type=F32) + b_ref[0]


def _ada_table(cond, w_ada, b_ada):
    n_out = 6 * D_MODEL
    vmem = 2 * (D_MODEL * ADA_TILE * 4) + D_MODEL * ADA_TILE * 2 + 4 * MIB
    return pl.pallas_call(
        _ada_kernel,
        grid=(DEPTH, n_out // ADA_TILE),
        in_specs=[
            pl.BlockSpec((MOD_ROWS, D_MODEL), lambda l, j: (0, 0)),
            pl.BlockSpec((1, D_MODEL, ADA_TILE), lambda l, j: (l, 0, j)),
            pl.BlockSpec((1, 1, ADA_TILE), lambda l, j: (l, 0, j)),
        ],
        out_specs=pl.BlockSpec((1, MOD_ROWS, ADA_TILE), lambda l, j: (l, 0, j)),
        out_shape=jax.ShapeDtypeStruct((DEPTH, MOD_ROWS, n_out), F32),
        compiler_params=_params(vmem, 2),
        name="ada_table",
    )(cond, w_ada, b_ada.reshape(DEPTH, 1, n_out))


def _normed(x):
    return x * lax.rsqrt(jnp.mean(x * x, axis=-1, keepdims=True) + EPS)


_CTX_BLOCKS = T_CTX // ELEM_ROWS
ELEM_CHUNK = 16


def _for_row_chunks(body):
    def step(c, carry):
        body(pl.ds(pl.multiple_of(c * ELEM_CHUNK, ELEM_CHUNK), ELEM_CHUNK))
        return carry
    lax.fori_loop(0, ELEM_ROWS // ELEM_CHUNK, step, 0, unroll=4)


def _rows(v):
    return jnp.broadcast_to(v, (ELEM_CHUNK, D_MODEL))


def _set_post_vec(rv_ref, k, gpost_ref, mod_ref, sub):
    rv_ref[k] = _rows(mod_ref[pl.ds(3 * sub + 2, 1), :] * gpost_ref[...])


def _set_pre_vecs(rv_ref, k, gpre_ref, mod_ref, sub):
    rv_ref[k] = _rows(gpre_ref[...] * (1.0 + mod_ref[pl.ds(3 * sub + 1, 1), :]))
    rv_ref[k + 1] = _rows(mod_ref[pl.ds(3 * sub, 1), :])


def _read_stream(yp_ref, ys_ref, rows):
    return jnp.where(pl.program_id(0) < _CTX_BLOCKS, yp_ref[rows, :], ys_ref[rows, :])


def _pre_kernel(sub, yp_ref, ys_ref, gpre_ref, mod_ref, h_ref, rv_ref):
    _set_pre_vecs(rv_ref, 0, gpre_ref, mod_ref, sub)

    def body(rows):
        h_ref[rows, :] = (_normed(_read_stream(yp_ref, ys_ref, rows)) * rv_ref[0] + rv_ref[1]).astype(h_ref.dtype)
    _for_row_chunks(body)


def _post_kernel(sub, o_ref, y_ref, gpost_ref, mod_ref, yp_ref, ys_ref, rv_ref):
    _set_post_vec(rv_ref, 0, gpost_ref, mod_ref, sub)

    def residual(rows):
        return y_ref[rows, :] + rv_ref[0] * _normed(o_ref[rows, :].astype(F32))

    @pl.when(pl.program_id(0) < _CTX_BLOCKS)
    def _():
        def body(rows):
            yp_ref[rows, :] = residual(rows)
        _for_row_chunks(body)

    @pl.when(pl.program_id(0) >= _CTX_BLOCKS)
    def _():
        def body(rows):
            ys_ref[rows, :] = residual(rows)
        _for_row_chunks(body)


def _postpre_kernel(sub, nsub, split_in, o_ref, *refs):
    if split_in:
        yp_ref, ys_ref, gpost_ref, mod_ref, gpre_ref, nmod_ref, yo_ref, h_ref, rv_ref = refs
    else:
        y_ref, gpost_ref, mod_ref, gpre_ref, nmod_ref, yo_ref, h_ref, rv_ref = refs
    _set_post_vec(rv_ref, 0, gpost_ref, mod_ref, sub)
    _set_pre_vecs(rv_ref, 1, gpre_ref, nmod_ref, nsub)

    def body(rows):
        y = _read_stream(yp_ref, ys_ref, rows) if split_in else y_ref[rows, :]
        yn = y + rv_ref[0] * _normed(o_ref[rows, :].astype(F32))
        yo_ref[rows, :] = yn
        h_ref[rows, :] = (_normed(yn) * rv_ref[1] + rv_ref[2]).astype(h_ref.dtype)
    _for_row_chunks(body)


def _rv_scratch(n):
    return [pltpu.VMEM((n, ELEM_CHUNK, D_MODEL), F32)]


def _row_spec():
    return pl.BlockSpec((ELEM_ROWS, D_MODEL), lambda i: (i, 0))


def _split_specs():
    return [pl.BlockSpec((ELEM_ROWS, D_MODEL), lambda i: (jnp.minimum(i, _CTX_BLOCKS - 1), 0)),
            pl.BlockSpec((ELEM_ROWS, D_MODEL), lambda i: (jnp.maximum(i - _CTX_BLOCKS, 0), 0))]


def _gain_spec(layer):
    return pl.BlockSpec((None, 1, D_MODEL), lambda i: (layer, 0, 0))


def _mod_spec(layer):
    return pl.BlockSpec((None, None, 6, D_MODEL),
                        lambda i: (layer, _mod_row_of_block(i, ELEM_ROWS), 0, 0))


_ELEM_BLOCK_F32 = ELEM_ROWS * D_MODEL * 4


def _pre_call(xp, xs, g_pre, mod, layer, sub):
    vmem = 2 * (2 * _ELEM_BLOCK_F32 + _ELEM_BLOCK_F32 // 2) + 3 * _ELEM_BLOCK_F32 + 2 * MIB
    return pl.pallas_call(
        functools.partial(_pre_kernel, sub),
        grid=(T_ALL // ELEM_ROWS,),
        in_specs=_split_specs() + [_gain_spec(layer), _mod_spec(layer)],
        out_specs=_row_spec(),
        out_shape=jax.ShapeDtypeStruct((T_ALL, D_MODEL), BF16),
        scratch_shapes=_rv_scratch(2),
        compiler_params=_params(vmem, 1),
        name="pre_norm",
    )(xp, xs, g_pre, mod)


def _post_call(o, y, g_post, mod, layer, sub):
    vmem = 2 * 4 * _ELEM_BLOCK_F32 + 3 * _ELEM_BLOCK_F32 + 2 * MIB
    return pl.pallas_call(
        functools.partial(_post_kernel, sub),
        grid=(T_ALL // ELEM_ROWS,),
        in_specs=[_row_spec(), _row_spec(), _gain_spec(layer), _mod_spec(layer)],
        out_specs=_split_specs(),
        out_shape=[jax.ShapeDtypeStruct((T_CTX, D_MODEL), F32),
                   jax.ShapeDtypeStruct((T_LAT, D_MODEL), F32)],
        scratch_shapes=_rv_scratch(1),
        compiler_params=_params(vmem, 1),
        name="post_norm",
    )(o, y, g_post, mod)


def _postpre_call(o, y, g_post, mod, layer, sub, g_pre, nlayer, nsub):
    split_in = isinstance(y, tuple)
    y_args = y if split_in else (y,)
    y_specs = _split_specs() if split_in else [_row_spec()]
    vmem = 2 * ((2 + len(y_args)) * _ELEM_BLOCK_F32 + _ELEM_BLOCK_F32 // 2) + 4 * _ELEM_BLOCK_F32 + 2 * MIB
    return pl.pallas_call(
        functools.partial(_postpre_kernel, sub, nsub, split_in),
        grid=(T_ALL // ELEM_ROWS,),
        in_specs=[_row_spec()] + y_specs + [_gain_spec(layer), _mod_spec(layer),
                                            _gain_spec(nlayer), _mod_spec(nlayer)],
        out_specs=[_row_spec(), _row_spec()],
        out_shape=[jax.ShapeDtypeStruct((T_ALL, D_MODEL), F32),
                   jax.ShapeDtypeStruct((T_ALL, D_MODEL), BF16)],
        scratch_shapes=_rv_scratch(3),
        compiler_params=_params(vmem, 1),
        name="post_pre_norm",
    )(o, *y_args, g_post, mod, g_pre, mod)


STREAM_SLOTS = 2


def _aligned(v, m):
    return v if isinstance(v, int) else pl.multiple_of(v, m)


def _wstream_kernel(layer, seg_starts, seg_width, chunk_rows, n_chunks, epilogue,
                    x_ref, w_hbm, o_ref, wb0_ref, wb1_ref, stage_ref, sem):
    j = pl.program_id(0)
    i = pl.program_id(1)
    nj = pl.num_programs(0)
    ni = pl.num_programs(1)
    step = j * ni + i
    set_new = lax.rem(step, 2)
    set_old = 1 - set_new

    def chunk_copies(tile, c, set_, slot):
        r0 = _aligned(c * chunk_rows, chunk_rows)
        return [pltpu.make_async_copy(
            w_hbm.at[layer, pl.ds(r0, chunk_rows), pl.ds(_aligned(start(tile), seg_width), seg_width)],
            stage_ref.at[set_, slot, :, pl.ds(s * seg_width, seg_width)],
            sem.at[set_, slot]) for s, start in enumerate(seg_starts)]

    def group_chunk(group, slot):
        return jnp.minimum(group * STREAM_SLOTS + slot, n_chunks - 1)

    def start_group(tile, group, set_):
        for slot in range(STREAM_SLOTS):
            for cp in chunk_copies(tile, group_chunk(group, slot), set_, slot):
                cp.start(priority=1)

    def round_chunk(wb_ref, c, set_, slot):
        wb_ref[pl.ds(_aligned(c * chunk_rows, chunk_rows), chunk_rows), :] = stage_ref[set_, slot].astype(BF16)

    @pl.when(step == 0)
    def _():
        for cp in chunk_copies(0, 0, 0, 0):
            cp.start()
        for c in range(n_chunks):
            if c + 1 < n_chunks:
                for cp in chunk_copies(0, c + 1, (c + 1) % 2, 0):
                    cp.start()
            for cp in chunk_copies(0, c, c % 2, 0):
                cp.wait()
            round_chunk(wb0_ref, c, c % 2, 0)
        start_group(jnp.minimum(1, nj - 1), 0, 1)

    for slot in range(STREAM_SLOTS):
        for cp in chunk_copies(0, 0, set_old, slot):
            cp.wait()

    @pl.when(step < nj * ni - 1)
    def _():
        wrap = i == ni - 1
        tile = jnp.minimum(jnp.where(wrap, j + 2, j + 1), nj - 1)
        start_group(tile, jnp.where(wrap, 0, i + 1), set_new)

    for parity, (wb_cur, wb_idle) in enumerate(((wb0_ref, wb1_ref), (wb1_ref, wb0_ref))):
        @pl.when(lax.rem(j, 2) == parity)
        def _(wb_cur=wb_cur, wb_idle=wb_idle):
            for slot in range(STREAM_SLOTS):
                round_chunk(wb_idle, group_chunk(i, slot), set_old, slot)
            acc = jnp.dot(x_ref[...], wb_cur[...], preferred_element_type=F32)
            o_ref[...] = epilogue(acc).astype(o_ref.dtype)


def _matmul_wstream(x, w_stack, layer, *, rows, seg_starts, seg_width, out_cols, chunk_rows, out_dtype,
                    epilogue, name):
    m, k = x.shape
    cols = len(seg_starts) * seg_width
    n_tiles = w_stack.shape[2] // cols
    n_row_blocks = m // rows
    n_chunks = k // chunk_rows
    assert m % rows == 0 and k % chunk_rows == 0 and w_stack.shape[2] % cols == 0
    assert n_row_blocks * STREAM_SLOTS >= n_chunks >= 2
    out_bytes = jnp.dtype(out_dtype).itemsize
    vmem = (2 * rows * k * 2 + 2 * k * cols * 2 + 2 * STREAM_SLOTS * chunk_rows * cols * 4
            + 2 * rows * out_cols * out_bytes + 2 * rows * cols * 4 + 2 * MIB)
    return pl.pallas_call(
        functools.partial(_wstream_kernel, layer, seg_starts, seg_width, chunk_rows, n_chunks, epilogue),
        grid=(n_tiles, n_row_blocks),
        in_specs=[pl.BlockSpec((rows, k), lambda j, i: (i, 0)),
                  pl.BlockSpec(memory_space=pl.ANY)],
        out_specs=pl.BlockSpec((rows, out_cols), lambda j, i: (i, j)),
        out_shape=jax.ShapeDtypeStruct((m, n_tiles * out_cols), out_dtype),
        scratch_shapes=[pltpu.VMEM((k, cols), BF16), pltpu.VMEM((k, cols), BF16),
                        pltpu.VMEM((2, STREAM_SLOTS, chunk_rows, cols), F32),
                        pltpu.SemaphoreType.DMA((2, STREAM_SLOTS))],
        compiler_params=_params(vmem, 2),
        name=name,
    )(x, w_stack)


def _identity(acc):
    return acc


def _project(x, w_stack, layer, *, rows, cols, chunk_rows, out_dtype, name):
    return _matmul_wstream(x, w_stack, layer, rows=rows, seg_starts=[lambda t: t * cols], seg_width=cols,
                           out_cols=cols, chunk_rows=chunk_rows, out_dtype=out_dtype, epilogue=_identity,
                           name=name)


def _swiglu(acc):
    g = acc[:, :FFN_TILE]
    u = acc[:, FFN_TILE:]
    return g * jax.nn.sigmoid(g) * u


def _swiglu_in(h, w_in_stack, layer):
    return _matmul_wstream(
        h, w_in_stack, layer, rows=FFN_IN_ROWS,
        seg_starts=[lambda t: t * FFN_TILE, lambda t: D_FF + t * FFN_TILE], seg_width=FFN_TILE,
        out_cols=FFN_TILE, chunk_rows=512, out_dtype=BF16, epilogue=_swiglu, name="ffn_in_swiglu")


def _attn_ctx_kernel(sink_ref, q_ref, k_ref, v_ref, prev_ref, o_ref):
    del prev_ref
    head0 = pl.program_id(1) * (CTX_KV_PER_STEP * KV_GROUP)
    heads = []
    for kl in range(CTX_KV_PER_STEP):
        kv_cols = slice(kl * HEAD_DIM, (kl + 1) * HEAD_DIM)
        kt = k_ref[:, kv_cols].T.astype(BF16)
        v = v_ref[:, kv_cols].astype(BF16)
        for g in range(KV_GROUP):
            hl = kl * KV_GROUP + g
            heads.append((slice(hl * HEAD_DIM, (hl + 1) * HEAD_DIM), kt, v, sink_ref[head0 + hl] * LOG2E))
    scores = [jnp.dot((q_ref[:, cols] * Q_SCALE).astype(BF16), kt, preferred_element_type=F32)
              for cols, kt, _, _ in heads]
    probs = []
    for (_, _, _, sk), s in zip(heads, scores):
        m = jnp.maximum(jnp.max(_fold_lanes(s, jnp.maximum), axis=-1, keepdims=True), sk)
        p = jnp.exp2(s - m)
        denom = jnp.sum(_fold_lanes(p, jnp.add), axis=-1, keepdims=True) + jnp.exp2(sk - m)
        probs.append((p.astype(BF16), denom))
    for (cols, _, v, _), (p, denom) in zip(heads, probs):
        o = jnp.dot(p, v, preferred_element_type=F32)
        o_ref[:, cols] = (o / denom).astype(o_ref.dtype)


def _fold_lanes(x, op):
    tiles = [x[:, t * HEAD_DIM:(t + 1) * HEAD_DIM] for t in range(x.shape[1] // HEAD_DIM)]
    while len(tiles) > 1:
        tiles = [op(tiles[t], tiles[t + 1]) for t in range(0, len(tiles), 2)]
    return tiles[0]


def _rope(x, cos, sin_lo, sin_hi):
    return x * cos + pltpu.roll(x, 96, 1) * sin_lo + pltpu.roll(x, 32, 1) * sin_hi


def _attn_lat_kernel(sink_ref, q_ref, k_ref, v_ref, ck_ref, cv_ref, cos_ref, slo_ref, shi_ref,
                     cosq_ref, sloq_ref, shiq_ref, o_ref, kt_ref, vb_ref, ckt_ref, cvb_ref):
    kh = pl.program_id(1)
    n_blocks = LAT_LEN // ATTN_BLOCK
    kt = []
    for n in range(n_blocks):
        blk = slice(n * ATTN_BLOCK, (n + 1) * ATTN_BLOCK)
        kt.append(_rope(k_ref[blk, :], cos_ref[blk, :], slo_ref[blk, :], shi_ref[blk, :]).T.astype(BF16))
    pad = jnp.zeros((HEAD_DIM, ATTN_BLOCK), BF16)
    kt = [pad] + kt + [pad]
    for n in range(n_blocks):
        kt_ref[n] = jnp.concatenate(kt[n:n + 3], axis=1)
    vb_ref[:ATTN_BLOCK, :] = pad
    vb_ref[pl.ds(ATTN_BLOCK, LAT_LEN), :] = v_ref[...].astype(BF16)
    vb_ref[pl.ds(ATTN_BLOCK + LAT_LEN, ATTN_BLOCK), :] = pad
    head_rows = pl.ds(kh, PAST_LEN, stride=N_KV_HEADS)
    ckt_ref[...] = ck_ref[head_rows, :].T.astype(BF16)
    cvb_ref[...] = cv_ref[head_rows, :].astype(BF16)

    rows = KV_GROUP * ATTN_BLOCK
    kq = (lax.broadcasted_iota(jnp.int32, (rows, ATTN_BLOCK), 1)
          - lax.broadcasted_iota(jnp.int32, (rows, ATTN_BLOCK), 0) % ATTN_BLOCK)
    sk = jnp.concatenate(
        [jnp.full((ATTN_BLOCK, 1), sink_ref[kh * KV_GROUP + g] * LOG2E, F32) for g in range(KV_GROUP)],
        axis=0)

    def scores(n):
        r0 = pl.multiple_of(n * ATTN_BLOCK, ATTN_BLOCK)
        cos = cosq_ref[pl.ds(r0, ATTN_BLOCK), :]
        slo = sloq_ref[pl.ds(r0, ATTN_BLOCK), :]
        shi = shiq_ref[pl.ds(r0, ATTN_BLOCK), :]
        q = jnp.concatenate(
            [_rope(q_ref[pl.ds(r0, ATTN_BLOCK), g * HEAD_DIM:(g + 1) * HEAD_DIM], cos, slo, shi).astype(BF16)
             for g in range(KV_GROUP)], axis=0)
        thr_p = jnp.where(n >= 1, 0, 2 * ATTN_BLOCK)
        thr_n = jnp.where(n <= n_blocks - 2, 0, -2 * ATTN_BLOCK)
        s_w = jnp.dot(q, kt_ref[n], preferred_element_type=F32)
        s_p = jnp.where(kq >= thr_p, s_w[:, :ATTN_BLOCK], NEG)
        s_0 = s_w[:, ATTN_BLOCK:2 * ATTN_BLOCK]
        s_n = jnp.where(kq <= thr_n, s_w[:, 2 * ATTN_BLOCK:], NEG)
        s_c = jnp.dot(q, ckt_ref[...], preferred_element_type=F32)
        return s_p, s_0, s_n, s_c

    def softmax(s_p, s_0, s_n, s_c):
        m = jnp.maximum(jnp.maximum(s_p, s_0), jnp.maximum(s_n, _fold_lanes(s_c, jnp.maximum)))
        m = jnp.maximum(jnp.max(m, axis=-1, keepdims=True), sk)
        p_p = jnp.exp2(s_p - m)
        p_0 = jnp.exp2(s_0 - m)
        p_n = jnp.exp2(s_n - m)
        p_c = jnp.exp2(s_c - m)
        denom = (jnp.sum((p_p + p_0) + (p_n + _fold_lanes(p_c, jnp.add)), axis=-1, keepdims=True)
                 + jnp.exp2(sk - m))
        p_w = jnp.concatenate([p_p.astype(BF16), p_0.astype(BF16), p_n.astype(BF16)], axis=1)
        return p_w, p_c.astype(BF16), denom

    def weighted_values(n, p_w, p_c, denom):
        r0 = pl.multiple_of(n * ATTN_BLOCK, ATTN_BLOCK)
        o = (jnp.dot(p_w, vb_ref[pl.ds(r0, 3 * ATTN_BLOCK), :], preferred_element_type=F32)
             + jnp.dot(p_c, cvb_ref[...], preferred_element_type=F32))
        o = o / denom
        for g in range(KV_GROUP):
            o_ref[pl.ds(r0, ATTN_BLOCK), g * HEAD_DIM:(g + 1) * HEAD_DIM] = (
                o[g * ATTN_BLOCK:(g + 1) * ATTN_BLOCK, :].astype(o_ref.dtype))

    def group(i, carry):
        blocks = [i * ATTN_GROUP + j for j in range(ATTN_GROUP)]
        s = [scores(n) for n in blocks]
        p = [softmax(*sn) for sn in s]
        for n, pn in zip(blocks, p):
            weighted_values(n, *pn)
        return carry

    lax.fori_loop(0, n_blocks // ATTN_GROUP, group, 0)


def _rope_tables():
    pos = jnp.arange(LAT_LEN, dtype=jnp.int32)
    row = (pos // GRID_W).astype(F32)
    col = (pos % GRID_W).astype(F32)
    half = HEAD_DIM // 2
    inv_freq = ROPE_BASE ** (-jnp.arange(0, half, 2, dtype=F32) / half)
    ang_r = row[:, None] * inv_freq[None, :]
    ang_c = col[:, None] * inv_freq[None, :]
    zero = jnp.zeros_like(ang_r)
    cos = jnp.concatenate([jnp.cos(ang_r), jnp.cos(ang_r), jnp.cos(ang_c), jnp.cos(ang_c)], axis=1)
    sin_lo = jnp.concatenate([-jnp.sin(ang_r), zero, -jnp.sin(ang_c), zero], axis=1)
    sin_hi = jnp.concatenate([zero, jnp.sin(ang_r), zero, jnp.sin(ang_c)], axis=1)
    return cos, sin_lo, sin_hi


def _attention(qkv, cache_k, cache_v, sink, layer_a):
    q_cols = KV_GROUP * HEAD_DIM
    k_blk0 = N_HEADS
    v_blk0 = N_HEADS + N_KV_HEADS
    ctx_rows0 = T_CTX // LAT_LEN
    cos, sin_lo, sin_hi = _rope_tables()
    q_tabs = [t * Q_SCALE for t in (cos, sin_lo, sin_hi)]
    ck = cache_k.reshape(N_LAT_REQ, -1, PAST_LEN * N_KV_HEADS, HEAD_DIM)
    cv = cache_v.reshape(N_LAT_REQ, -1, PAST_LEN * N_KV_HEADS, HEAD_DIM)
    cache_spec = pl.BlockSpec((None, None, PAST_LEN * N_KV_HEADS, HEAD_DIM), lambda b, h: (b, layer_a, 0, 0))
    tab_spec = pl.BlockSpec((LAT_LEN, HEAD_DIM), lambda b, h: (0, 0))
    smem_spec = pl.BlockSpec(memory_space=pltpu.SMEM)
    lat_block = LAT_LEN * HEAD_DIM * 4
    cache_blocks = 2 * PAST_LEN * N_KV_HEADS // LAT_LEN
    vmem_lat = (2 * (q_cols // HEAD_DIM + 2 + cache_blocks + 6) * lat_block + 2 * LAT_LEN * q_cols * 2
                + 3 * lat_block + 16 * MIB)
    o_lat = pl.pallas_call(
        _attn_lat_kernel,
        grid=(N_LAT_REQ, N_KV_HEADS),
        in_specs=[
            smem_spec,
            pl.BlockSpec((LAT_LEN, q_cols), lambda b, h: (ctx_rows0 + b, h)),
            pl.BlockSpec((LAT_LEN, HEAD_DIM), lambda b, h: (ctx_rows0 + b, k_blk0 + h)),
            pl.BlockSpec((LAT_LEN, HEAD_DIM), lambda b, h: (ctx_rows0 + b, v_blk0 + h)),
            cache_spec, cache_spec,
            tab_spec, tab_spec, tab_spec, tab_spec, tab_spec, tab_spec,
        ],
        out_specs=pl.BlockSpec((LAT_LEN, q_cols), lambda b, h: (ctx_rows0 + b, h)),
        out_shape=jax.ShapeDtypeStruct((T_ALL, N_HEADS * HEAD_DIM), BF16),
        scratch_shapes=[pltpu.VMEM((LAT_LEN // ATTN_BLOCK, HEAD_DIM, 3 * ATTN_BLOCK), BF16),
                        pltpu.VMEM((LAT_LEN + 2 * ATTN_BLOCK, HEAD_DIM), BF16),
                        pltpu.VMEM((HEAD_DIM, PAST_LEN), BF16), pltpu.VMEM((PAST_LEN, HEAD_DIM), BF16)],
        compiler_params=_params(vmem_lat, 2),
        name="attn_latent",
    )(sink, qkv, qkv, qkv, ck, cv, cos, sin_lo, sin_hi, *q_tabs)

    cq = CTX_KV_PER_STEP * q_cols
    ckv = CTX_KV_PER_STEP * HEAD_DIM
    vmem_ctx = 2 * (CTX_LEN * cq * 4 + 2 * CTX_LEN * ckv * 4 + CTX_LEN * cq * 2) + 16 * MIB
    return pl.pallas_call(
        _attn_ctx_kernel,
        grid=(N_CTX_REQ, N_KV_HEADS // CTX_KV_PER_STEP),
        in_specs=[
            smem_spec,
            pl.BlockSpec((CTX_LEN, cq), lambda b, h: (b, h)),
            pl.BlockSpec((CTX_LEN, ckv), lambda b, h: (b, k_blk0 // CTX_KV_PER_STEP + h)),
            pl.BlockSpec((CTX_LEN, ckv), lambda b, h: (b, v_blk0 // CTX_KV_PER_STEP + h)),
            pl.BlockSpec(memory_space=pl.ANY),
        ],
        out_specs=pl.BlockSpec((CTX_LEN, cq), lambda b, h: (b, h)),
        out_shape=jax.ShapeDtypeStruct((T_ALL, N_HEADS * HEAD_DIM), BF16),
        input_output_aliases={4: 0},
        compiler_params=_params(vmem_ctx, 2),
        name="attn_context",
    )(sink, qkv, qkv, qkv, o_lat)


def _dft_tables(n):
    k = jnp.arange(n, dtype=jnp.int32)
    ang = ((k[:, None] * k[None, :]) % n).astype(F32) * (2.0 * math.pi / n)
    return jnp.cos(ang), jnp.sin(ang)


def _fourier_group(x, cs_ref, pos_ref, norm):
    z = jnp.dot(x, cs_ref[...], preferred_element_type=F32)
    zz = jnp.concatenate([z[:, :GROUP_DIM], z[:, GROUP_DIM:]], axis=0).astype(BF16)
    return jnp.dot(pos_ref[...], zz, preferred_element_type=F32) * norm


def _fourier_lat_kernel(x_ref, cs_ref, pos_ref, o_ref):
    norm = (LAT_LEN * GROUP_DIM) ** -0.5
    o_ref[...] = _fourier_group(x_ref[...], cs_ref, pos_ref, norm).astype(o_ref.dtype)


def _fourier_ctx_kernel(x_ref, cs_ref, pos_ref, prev_ref, o_ref):
    del prev_ref
    norm = (CTX_LEN * GROUP_DIM) ** -0.5
    for g in range(N_GROUPS):
        cols = slice(g * GROUP_DIM, (g + 1) * GROUP_DIM)
        o_ref[:, cols] = _fourier_group(x_ref[:, cols], cs_ref, pos_ref, norm).astype(o_ref.dtype)


def _fourier(h):
    cc, sc = _dft_tables(GROUP_DIM)
    cs = jnp.concatenate([cc, sc], axis=1).astype(BF16)
    cl, sl = _dft_tables(LAT_LEN)
    pos_lat = jnp.concatenate([cl, -sl], axis=1).astype(BF16)
    cp, sp = _dft_tables(CTX_LEN)
    pos_ctx = jnp.concatenate([cp, -sp], axis=1).astype(BF16)
    ctx_rows0 = T_CTX // LAT_LEN

    blk = LAT_LEN * GROUP_DIM
    vmem_lat = 2 * (2 * blk * 2 + GROUP_DIM * 2 * GROUP_DIM * 2 + LAT_LEN * 2 * LAT_LEN * 2) + 5 * blk * 4 + 4 * MIB
    f_lat = pl.pallas_call(
        _fourier_lat_kernel,
        grid=(N_LAT_REQ, N_GROUPS),
        in_specs=[pl.BlockSpec((LAT_LEN, GROUP_DIM), lambda b, g: (ctx_rows0 + b, g)),
                  pl.BlockSpec((GROUP_DIM, 2 * GROUP_DIM), lambda b, g: (0, 0)),
                  pl.BlockSpec((LAT_LEN, 2 * LAT_LEN), lambda b, g: (0, 0))],
        out_specs=pl.BlockSpec((LAT_LEN, GROUP_DIM), lambda b, g: (ctx_rows0 + b, g)),
        out_shape=jax.ShapeDtypeStruct((T_ALL, D_MODEL), BF16),
        compiler_params=_params(vmem_lat, 2),
        name="fourier_latent",
    )(h, cs, pos_lat)

    vmem_ctx = 2 * (2 * CTX_LEN * D_MODEL * 2 + GROUP_DIM * 2 * GROUP_DIM * 2 + CTX_LEN * 2 * CTX_LEN * 2) + 16 * MIB
    return pl.pallas_call(
        _fourier_ctx_kernel,
        grid=(N_CTX_REQ,),
        in_specs=[pl.BlockSpec((CTX_LEN, D_MODEL), lambda b: (b, 0)),
                  pl.BlockSpec((GROUP_DIM, 2 * GROUP_DIM), lambda b: (0, 0)),
                  pl.BlockSpec((CTX_LEN, 2 * CTX_LEN), lambda b: (0, 0)),
                  pl.BlockSpec(memory_space=pl.ANY)],
        out_specs=pl.BlockSpec((CTX_LEN, D_MODEL), lambda b: (b, 0)),
        out_shape=jax.ShapeDtypeStruct((T_ALL, D_MODEL), BF16),
        input_output_aliases={3: 0},
        compiler_params=_params(vmem_ctx, 1),
        name="fourier_context",
    )(h, cs, pos_ctx, f_lat)


def _cache_kernel(*refs):
    n_layers = (len(refs) - 2) // 2
    k_refs, v_refs = refs[:n_layers], refs[n_layers:2 * n_layers]
    ko_ref, vo_ref = refs[2 * n_layers:]
    for a in range(n_layers):
        @pl.when(pl.program_id(1) == a)
        def _(a=a):
            ko_ref[...] = k_refs[a][...].reshape(CTX_LEN, N_KV_HEADS, HEAD_DIM)
            vo_ref[...] = v_refs[a][...].reshape(CTX_LEN, N_KV_HEADS, HEAD_DIM)


def _new_caches(qkvs):
    n_layers = len(qkvs)
    kv_cols = N_KV_HEADS * HEAD_DIM
    k_blk = N_HEADS * HEAD_DIM // kv_cols
    shape = jax.ShapeDtypeStruct((N_CTX_REQ, n_layers, CTX_LEN, N_KV_HEADS, HEAD_DIM), F32)
    out_spec = pl.BlockSpec((None, None, CTX_LEN, N_KV_HEADS, HEAD_DIM), lambda b, a: (b, a, 0, 0, 0))
    vmem = 2 * (2 * n_layers + 2) * CTX_LEN * kv_cols * 4 + 4 * MIB
    return pl.pallas_call(
        _cache_kernel,
        grid=(N_CTX_REQ, n_layers),
        in_specs=([pl.BlockSpec((CTX_LEN, kv_cols), lambda b, a: (b, k_blk))] * n_layers
                  + [pl.BlockSpec((CTX_LEN, kv_cols), lambda b, a: (b, k_blk + 1))] * n_layers),
        out_specs=[out_spec, out_spec],
        out_shape=[shape, shape],
        compiler_params=_params(vmem, 2),
        name="new_caches",
    )(*qkvs, *qkvs)


def kernel(x_prompt, x_sample, cache_k, cache_v, c, c_ctx, w_ada, b_ada, g_mix_pre, g_mix_post,
           g_ffn_pre, g_ffn_post, w_qkv, w_attn_out, attn_sink, w_fourier, w_ffn_in, w_ffn_out):
    xp = x_prompt.reshape(T_CTX, D_MODEL)
    xs = x_sample.reshape(T_LAT, D_MODEL)
    cond = jnp.concatenate(
        [c, c_ctx[None, :], jnp.zeros((MOD_ROWS - N_LAT_REQ - 1, D_MODEL), F32)], axis=0)
    mod = _ada_table(cond, w_ada, b_ada).reshape(DEPTH, MOD_ROWS, 6, D_MODEL)
    g_mix_pre, g_mix_post, g_ffn_pre, g_ffn_post = (
        g.reshape(DEPTH, 1, D_MODEL) for g in (g_mix_pre, g_mix_post, g_ffn_pre, g_ffn_post))
    qkvs = []
    y = (xp, xs)
    h = _pre_call(xp, xs, g_mix_pre, mod, 0, 0)
    for i in range(DEPTH):
        if i % 2 == 0:
            a = i // 2
            qkv = _project(h, w_qkv, a, rows=MM_ROWS, cols=1024, chunk_rows=256, out_dtype=F32,
                           name="qkv_proj")
            qkvs.append(qkv)
            att = _attention(qkv, cache_k, cache_v, attn_sink[a], a)
            o = _project(att, w_attn_out, a, rows=MM_ROWS, cols=1024, chunk_rows=256, out_dtype=BF16,
                         name="attn_out_proj")
        else:
            o = _project(_fourier(h), w_fourier, i // 2, rows=MM_ROWS, cols=1024, chunk_rows=256,
                         out_dtype=BF16, name="fourier_proj")
        y, h = _postpre_call(o, y, g_mix_post, mod, i, 0, g_ffn_pre, i, 1)
        act = _swiglu_in(h, w_ffn_in, i)
        o = _project(act, w_ffn_out, i, rows=FFN_OUT_ROWS, cols=512, chunk_rows=256, out_dtype=BF16,
                     name="ffn_out_proj")
        if i + 1 < DEPTH:
            y, h = _postpre_call(o, y, g_ffn_post, mod, i, 1, g_mix_pre, i + 1, 0)
        else:
            y_p, y_s = _post_call(o, y, g_ffn_post, mod, i, 1)

    new_k, new_v = _new_caches(qkvs)
    return (y_p.reshape(N_CTX_REQ, CTX_LEN, D_MODEL), y_s.reshape(N_LAT_REQ, LAT_LEN, D_MODEL), new_k, new_v)
```
